```python
import jax, jax.numpy as jnp
from jax import lax
import numpy as np

D_MODEL = 2048
BATCH = 4
SEQ = 4096
DEPTH = 2

GRID_W = 64
CTX_LEN = 256
HEAD_DIM = 128
AXIS_DIM = HEAD_DIM // 2
ATT_Q_HEADS = 12
ATT_KV_HEADS = 4
ATT_GROUPS = ATT_Q_HEADS // ATT_KV_HEADS
ATT_WIDTH = ATT_Q_HEADS * HEAD_DIM
KV_WIDTH = ATT_KV_HEADS * HEAD_DIM
AUX_WIDTH = D_MODEL - ATT_WIDTH
AUX_GROUPS = 4
AUX_GROUP_DIM = AUX_WIDTH // AUX_GROUPS
CONV_WIDTH = 3
POOL_WINDOWS = (2, 4, 8, 16)
WINDOW = 128
Q_BLOCK = 128
BAND = Q_BLOCK + 2 * WINDOW
ROPE_THETA = 10000.0
FFN_HIDDEN = -(-8 * D_MODEL // (3 * 256)) * 256
N_MOD = 6
EPS = 1e-6
NEG_INF = -1e30
ATT_SCALE = HEAD_DIM ** -0.5
IN_WIDTHS_AB = (ATT_WIDTH, KV_WIDTH, KV_WIDTH, AUX_WIDTH, AUX_WIDTH, AUX_WIDTH)
IN_WIDTHS_CD = (ATT_WIDTH, KV_WIDTH, KV_WIDTH, AUX_WIDTH)

kernel_name = "hybrid_prefix_dit_block"


def _split_points(widths):
    return [int(v) for v in np.cumsum(widths)[:-1]]


def _rms_norm(x, g):
    xf = x.astype(jnp.float32)
    y = xf * lax.rsqrt(jnp.mean(xf * xf, axis=-1, keepdims=True) + EPS)
    return (y * g.astype(jnp.float32)).astype(x.dtype)


def _modulate(h, shift, scale):
    return h * (1 + scale) + shift


def _axial_rope(n):
    rows = n // GRID_W
    row = jnp.broadcast_to(jnp.arange(rows, dtype=jnp.float32)[:, None], (rows, GRID_W)).reshape(-1)
    col = jnp.broadcast_to(jnp.arange(GRID_W, dtype=jnp.float32)[None, :], (rows, GRID_W)).reshape(-1)
    inv = jnp.power(ROPE_THETA, -jnp.arange(0, AXIS_DIM, 2, dtype=jnp.float32) / AXIS_DIM)
    ang_r = row[:, None] * inv
    ang_c = col[:, None] * inv
    return (jnp.cos(ang_r), jnp.sin(ang_r), jnp.cos(ang_c), jnp.sin(ang_c))


def _rotate(x, cos, sin):
    x1, x2 = jnp.split(x, 2, axis=-1)
    c = cos[:, None, :].astype(x.dtype)
    s = sin[:, None, :].astype(x.dtype)
    return jnp.concatenate([x1 * c - x2 * s, x2 * c + x1 * s], axis=-1)


def _apply_axial_rope(x, rope):
    cr, sr, cc, sc = rope
    return jnp.concatenate([_rotate(x[..., :AXIS_DIM], cr, sr), _rotate(x[..., AXIS_DIM:], cc, sc)], axis=-1)


def _softmax_attend(q, k, v, bias=None, sink=None):
    s = jnp.einsum('bqkgd,bnkd->bkgqn', q, k).astype(jnp.float32) * ATT_SCALE
    if bias is not None:
        s = s + bias
    if sink is not None:
        sink_col = jnp.broadcast_to(sink.astype(jnp.float32)[None, :, :, None, None], s.shape[:-1] + (1,))
        p = jax.nn.softmax(jnp.concatenate([s, sink_col], axis=-1), axis=-1)[..., :-1]
    else:
        p = jax.nn.softmax(s, axis=-1)
    return jnp.einsum('bkgqn,bnkd->bqkgd', p.astype(v.dtype), v)


def _dense_latent_attention(q, k, v, kc, vc):
    B, S = q.shape[0], q.shape[1]
    nb = S // Q_BLOCK
    k_all = jnp.concatenate([kc, k], axis=1)
    v_all = jnp.concatenate([vc, v], axis=1)
    qb = jnp.moveaxis(q.reshape(B, nb, Q_BLOCK, ATT_KV_HEADS, ATT_GROUPS, HEAD_DIM), 1, 0)
    o = lax.map(lambda qi: _softmax_attend(qi, k_all, v_all), qb)
    return jnp.moveaxis(o, 0, 1).reshape(B, S, ATT_WIDTH)


def _window_latent_attention(q, k, v, kc, vc, sink):
    B, S = q.shape[0], q.shape[1]
    nb = S // Q_BLOCK
    pad = ((0, 0), (WINDOW, WINDOW), (0, 0), (0, 0))
    kp = jnp.pad(k, pad)
    vp = jnp.pad(v, pad)
    qb = jnp.moveaxis(q.reshape(B, nb, Q_BLOCK, ATT_KV_HEADS, ATT_GROUPS, HEAD_DIM), 1, 0)
    ctx_bias = jnp.zeros((Q_BLOCK, kc.shape[1]), jnp.float32)

    def block(args):
        qi, bi = args
        start = bi * Q_BLOCK
        kb = lax.dynamic_slice_in_dim(kp, start, BAND, axis=1)
        vb = lax.dynamic_slice_in_dim(vp, start, BAND, axis=1)
        kpos = start - WINDOW + jnp.arange(BAND, dtype=jnp.int32)
        qpos = start + jnp.arange(Q_BLOCK, dtype=jnp.int32)
        ok = (jnp.abs(kpos[None, :] - qpos[:, None]) <= WINDOW) & (kpos[None, :] >= 0) & (kpos[None, :] < S)
        band_bias = jnp.where(ok, jnp.float32(0.0), jnp.float32(NEG_INF))
        bias = jnp.concatenate([ctx_bias, band_bias], axis=-1)
        return _softmax_attend(qi, jnp.concatenate([kc, kb], axis=1), jnp.concatenate([vc, vb], axis=1), bias, sink)

    o = lax.map(block, (qb, jnp.arange(nb, dtype=jnp.int32)))
    return jnp.moveaxis(o, 0, 1).reshape(B, S, ATT_WIDTH)


def _short_conv(u, w):
    L = u.shape[1]
    half = CONV_WIDTH // 2
    up = jnp.pad(u, ((0, 0), (half, half), (0, 0)))
    out = w[0] * up[:, 0:L]
    for j in range(1, CONV_WIDTH):
        out = out + w[j] * up[:, j:j + L]
    return out


def _multiscale_pool(u, pool_w, pool_scale):
    B, L, _ = u.shape
    ug = u.reshape(B, L, AUX_GROUPS, AUX_GROUP_DIM)
    t = jnp.arange(L, dtype=jnp.int32)
    outs = []
    for g, w in enumerate(POOL_WINDOWS):
        ui = ug[:, :, g].astype(jnp.float32)
        cs = jnp.pad(jnp.cumsum(ui, axis=1), ((0, 0), (1, 0), (0, 0)))
        lo = jnp.clip(t - w // 2, 0, L)
        hi = jnp.clip(t - w // 2 + w, 0, L)
        mean = (cs[:, hi] - cs[:, lo]) / (hi - lo).astype(jnp.float32)[None, :, None]
        outs.append((mean - ui).astype(u.dtype))
    pooled = jnp.stack(outs, axis=2)
    mixed = jnp.einsum('blgc,gcd->blgd', pooled, pool_w).reshape(B, L, AUX_WIDTH)
    return mixed * pool_scale


def _heads(t, n_heads):
    return t.reshape(t.shape[0], t.shape[1], n_heads, HEAD_DIM)


def _ctx_kv(cn, w_in, k_g):
    kv = cn @ w_in[:, ATT_WIDTH:ATT_WIDTH + 2 * KV_WIDTH]
    kc, vc = jnp.split(kv, 2, axis=-1)
    return _rms_norm(_heads(kc, ATT_KV_HEADS), k_g), _heads(vc, ATT_KV_HEADS)


def _mixer_ab(xn, cn, p, rope, need_ctx):
    B, S, _ = xn.shape
    Lc = cn.shape[1]
    q, k, v, gb, gc, u = jnp.split(xn @ p['w_in'], _split_points(IN_WIDTHS_AB), axis=-1)
    q = _apply_axial_rope(_rms_norm(_heads(q, ATT_Q_HEADS), p['q_g']), rope)
    k = _apply_axial_rope(_rms_norm(_heads(k, ATT_KV_HEADS), p['k_g']), rope)
    v = _heads(v, ATT_KV_HEADS)
    kc, vc = _ctx_kv(cn, p['w_in'], p['k_g'])
    attn = _dense_latent_attention(q, k, v, kc, vc)
    conv = gb * _short_conv(gc * u, p['conv_w'])
    y = jnp.concatenate([attn, conv], axis=-1) @ p['w_out']
    yc = None
    if need_ctx:
        qc, _, _, gbc, gcc, uc = jnp.split(cn @ p['w_in'], _split_points(IN_WIDTHS_AB), axis=-1)
        qc = _rms_norm(_heads(qc, ATT_Q_HEADS), p['q_g']).reshape(B, Lc, ATT_KV_HEADS, ATT_GROUPS, HEAD_DIM)
        attn_c = _softmax_attend(qc, kc, vc).reshape(B, Lc, ATT_WIDTH)
        conv_c = gbc * _short_conv(gcc * uc, p['conv_w'])
        yc = jnp.concatenate([attn_c, conv_c], axis=-1) @ p['w_out']
    return y, yc


def _mixer_cd(xn, cn, p, rope, need_ctx):
    B, S, _ = xn.shape
    Lc = cn.shape[1]
    sink = p['sink'].reshape(ATT_KV_HEADS, ATT_GROUPS)
    q, k, v, u = jnp.split(xn @ p['w_in'], _split_points(IN_WIDTHS_CD), axis=-1)
    q = _apply_axial_rope(_rms_norm(_heads(q, ATT_Q_HEADS), p['q_g']), rope)
    k = _apply_axial_rope(_rms_norm(_heads(k, ATT_KV_HEADS), p['k_g']), rope)
    v = _heads(v, ATT_KV_HEADS)
    kc, vc = _ctx_kv(cn, p['w_in'], p['k_g'])
    attn = _window_latent_attention(q, k, v, kc, vc, sink)
    pool = _multiscale_pool(u, p['pool_w'], p['pool_scale'])
    y = jnp.concatenate([attn, pool], axis=-1) @ p['w_out']
    yc = None
    if need_ctx:
        qc, _, _, uc = jnp.split(cn @ p['w_in'], _split_points(IN_WIDTHS_CD), axis=-1)
        qc = _rms_norm(_heads(qc, ATT_Q_HEADS), p['q_g']).reshape(B, Lc, ATT_KV_HEADS, ATT_GROUPS, HEAD_DIM)
        attn_c = _softmax_attend(qc, kc, vc, None, sink).reshape(B, Lc, ATT_WIDTH)
        pool_c = _multiscale_pool(uc, p['pool_w'], p['pool_scale'])
        yc = jnp.concatenate([attn_c, pool_c], axis=-1) @ p['w_out']
    return y, yc


def _swiglu(h, w_gate, w_up, w_down):
    return (jax.nn.silu(h @ w_gate) * (h @ w_up)) @ w_down


def setup_inputs(seed: int = 0) -> dict:
    key = jax.random.key(seed)
    keys = iter(jax.random.split(key, 64))
    f32 = jnp.float32

    def nrm(shape, scale):
        return jax.random.normal(next(keys), shape, f32) * scale

    def gain(n):
        return jnp.ones((n,), f32) + nrm((n,), 0.02)

    d_in_ab = sum(IN_WIDTHS_AB)
    d_in_cd = sum(IN_WIDTHS_CD)
    inp = {}
    inp['x'] = nrm((BATCH, SEQ, D_MODEL), 1.0)
    inp['c'] = nrm((BATCH, D_MODEL), 1.0)
    inp['ctx'] = nrm((BATCH, CTX_LEN, D_MODEL), 1.0)
    inp['c_ctx'] = nrm((D_MODEL,), 1.0)
    inp['l0_norm1_g'] = gain(D_MODEL)
    inp['l0_w_mod'] = nrm((D_MODEL, N_MOD * D_MODEL), 0.5 * D_MODEL ** -0.5)
    inp['l0_b_mod'] = nrm((N_MOD * D_MODEL,), 0.02)
    inp['l0_w_in'] = nrm((D_MODEL, d_in_ab), D_MODEL ** -0.5)
    inp['l0_q_norm_g'] = gain(HEAD_DIM)
    inp['l0_k_norm_g'] = gain(HEAD_DIM)
    inp['l0_conv_w'] = nrm((CONV_WIDTH, AUX_WIDTH), CONV_WIDTH ** -0.5)
    inp['l0_w_out'] = nrm((D_MODEL, D_MODEL), D_MODEL ** -0.5)
    inp['l0_norm2_g'] = gain(D_MODEL)
    inp['l0_w_gate'] = nrm((D_MODEL, FFN_HIDDEN), D_MODEL ** -0.5)
    inp['l0_w_up'] = nrm((D_MODEL, FFN_HIDDEN), D_MODEL ** -0.5)
    inp['l0_w_down'] = nrm((FFN_HIDDEN, D_MODEL), FFN_HIDDEN ** -0.5)
    inp['l1_norm1_g'] = gain(D_MODEL)
    inp['l1_w_mod'] = nrm((D_MODEL, N_MOD * D_MODEL), 0.5 * D_MODEL ** -0.5)
    inp['l1_b_mod'] = nrm((N_MOD * D_MODEL,), 0.02)
    inp['l1_w_in'] = nrm((D_MODEL, d_in_cd), D_MODEL ** -0.5)
    inp['l1_q_norm_g'] = gain(HEAD_DIM)
    inp['l1_k_norm_g'] = gain(HEAD_DIM)
    inp['l1_sink'] = nrm((ATT_Q_HEADS,), 0.5)
    inp['l1_pool_w'] = nrm((AUX_GROUPS, AUX_GROUP_DIM, AUX_GROUP_DIM), AUX_GROUP_DIM ** -0.5)
    inp['l1_pool_scale'] = gain(AUX_WIDTH)
    inp['l1_w_out'] = nrm((D_MODEL, D_MODEL), D_MODEL ** -0.5)
    inp['l1_norm2_g'] = gain(D_MODEL)
    inp['l1_w_gate'] = nrm((D_MODEL, FFN_HIDDEN), D_MODEL ** -0.5)
    inp['l1_w_up'] = nrm((D_MODEL, FFN_HIDDEN), D_MODEL ** -0.5)
    inp['l1_w_down'] = nrm((FFN_HIDDEN, D_MODEL), FFN_HIDDEN ** -0.5)
    inp['final_norm_g'] = gain(D_MODEL)
    return inp


def reference(x, c, ctx, c_ctx,
              l0_norm1_g, l0_w_mod, l0_b_mod, l0_w_in, l0_q_norm_g, l0_k_norm_g, l0_conv_w, l0_w_out,
              l0_norm2_g, l0_w_gate, l0_w_up, l0_w_down,
              l1_norm1_g, l1_w_mod, l1_b_mod, l1_w_in, l1_q_norm_g, l1_k_norm_g, l1_sink, l1_pool_w,
              l1_pool_scale, l1_w_out, l1_norm2_g, l1_w_gate, l1_w_up, l1_w_down,
              final_norm_g):
    layers = [
        dict(norm1_g=l0_norm1_g, w_mod=l0_w_mod, b_mod=l0_b_mod, w_in=l0_w_in, q_g=l0_q_norm_g,
             k_g=l0_k_norm_g, conv_w=l0_conv_w, w_out=l0_w_out, norm2_g=l0_norm2_g,
             w_gate=l0_w_gate, w_up=l0_w_up, w_down=l0_w_down),
        dict(norm1_g=l1_norm1_g, w_mod=l1_w_mod, b_mod=l1_b_mod, w_in=l1_w_in, q_g=l1_q_norm_g,
             k_g=l1_k_norm_g, sink=l1_sink, pool_w=l1_pool_w, pool_scale=l1_pool_scale, w_out=l1_w_out,
             norm2_g=l1_norm2_g, w_gate=l1_w_gate, w_up=l1_w_up, w_down=l1_w_down),
    ]
    rope = _axial_rope(x.shape[1])
    h = x
    hc = ctx
    sc = jax.nn.silu(c)
    sc_ctx = jax.nn.silu(c_ctx)
    for i in range(DEPTH):
        p = layers[i]
        need_ctx = i < DEPTH - 1
        shift1, scale1, gate1, shift2, scale2, gate2 = jnp.split(sc @ p['w_mod'] + p['b_mod'], N_MOD, axis=-1)
        cshift1, cscale1, cgate1, cshift2, cscale2, cgate2 = jnp.split(sc_ctx @ p['w_mod'] + p['b_mod'], N_MOD, axis=-1)
        xn = _modulate(_rms_norm(h, p['norm1_g']), shift1[:, None], scale1[:, None])
        cn = _modulate(_rms_norm(hc, p['norm1_g']), cshift1, cscale1)
        if i % 2 == 0:
            y, yc = _mixer_ab(xn, cn, p, rope, need_ctx)
        else:
            y, yc = _mixer_cd(xn, cn, p, rope, need_ctx)
        h = h + gate1[:, None] * y
        hn = _modulate(_rms_norm(h, p['norm2_g']), shift2[:, None], scale2[:, None])
        h = h + gate2[:, None] * _swiglu(hn, p['w_gate'], p['w_up'], p['w_down'])
        if need_ctx:
            hc = hc + cgate1 * yc
            hcn = _modulate(_rms_norm(hc, p['norm2_g']), cshift2, cscale2)
            hc = hc + cgate2 * _swiglu(hcn, p['w_gate'], p['w_up'], p['w_down'])
    return _rms_norm(h, final_norm_g)
```

```python
import functools

import jax
import jax.numpy as jnp
import numpy as np
from jax import lax
from jax.experimental import pallas as pl
from jax.experimental.pallas import tpu as pltpu

F32 = jnp.float32
BF16 = jnp.bfloat16

LANES = 128
HEAD_DIM = 128
AXIS_DIM = HEAD_DIM // 2
Q_HEADS = 12
KV_HEADS = 4
GROUPS = Q_HEADS // KV_HEADS
ATT_WIDTH = Q_HEADS * HEAD_DIM
KV_WIDTH = KV_HEADS * HEAD_DIM
GRID_W = 64
WINDOW = 128
POOL_WINDOWS = (2, 4, 8, 16)
ROPE_THETA = 10000.0
N_MOD = 6
EPS = 1e-6
NEG_INF = -1e30
ATT_SCALE = HEAD_DIM ** -0.5
MOD_ROWS = 8
VMEM_LIMIT = 56 * 1024 * 1024


def _params(sem):
    return pltpu.CompilerParams(dimension_semantics=sem, vmem_limit_bytes=VMEM_LIMIT)


def _mod_spec(row0, chunk):
    def imap(b, *_):
        return (b + row0, 0, chunk)
    return imap


def _mod_kernel(c_ref, w_ref, b_ref, o_ref):
    c = c_ref[...]
    sc = c * jax.nn.sigmoid(c)
    o_ref[...] = jnp.dot(sc.astype(BF16), w_ref[...].astype(BF16),
                         preferred_element_type=F32) + b_ref[...]


def _modulation(c_all, w_mod, b_mod):
    d, n = w_mod.shape
    tn = 1024
    return pl.pallas_call(
        _mod_kernel,
        grid=(n // tn,),
        in_specs=[pl.BlockSpec((MOD_ROWS, d), lambda j: (0, 0)),
                  pl.BlockSpec((d, tn), lambda j: (0, j)),
                  pl.BlockSpec((1, tn), lambda j: (0, j))],
        out_specs=pl.BlockSpec((MOD_ROWS, tn), lambda j: (0, j)),
        out_shape=jax.ShapeDtypeStruct((MOD_ROWS, n), F32),
        name="modulation",
        compiler_params=_params(("arbitrary",)),
    )(c_all, w_mod, b_mod.reshape(1, n))


def _rms_modulate(x, g, shift, scale):
    ms = jnp.mean(x * x, axis=-1, keepdims=True)
    y = x * lax.rsqrt(ms + EPS) * g
    return y * (1.0 + scale) + shift


def _head_norm_rope(y, g, cos, sin):
    lane = lax.broadcasted_iota(jnp.int32, (1, HEAD_DIM), 1)
    first_half = (lane % AXIS_DIM) < (AXIS_DIM // 2)
    outs = []
    for hh in range(y.shape[1] // HEAD_DIM):
        t = y[:, hh * HEAD_DIM:(hh + 1) * HEAD_DIM]
        ms = jnp.mean(t * t, axis=-1, keepdims=True)
        t = t * lax.rsqrt(ms + EPS) * g
        if cos is not None:
            partner = jnp.where(first_half, pltpu.roll(t, HEAD_DIM - AXIS_DIM // 2, 1),
                                pltpu.roll(t, AXIS_DIM // 2, 1))
            t = t * cos + partner * sin
        outs.append(t)
    return jnp.concatenate(outs, axis=1)


def _inproj_kernel(*refs, j0, nq, rope, want_q, want_aux):
    it = iter(refs)
    h_ref, shift_ref, scale_ref, g1_ref, w_ref, qg_ref, kg_ref = (next(it) for _ in range(7))
    cos_ref = sin_ref = None
    if rope:
        cos_ref, sin_ref = next(it), next(it)
    q_ref = next(it) if want_q else None
    k_ref, v_ref = next(it), next(it)
    aux_ref = next(it) if want_aux else None
    xn_ref = next(it)

    jj = pl.program_id(2)
    j = jj + j0

    @pl.when(jj == 0)
    def _():
        xn_ref[...] = _rms_modulate(h_ref[...], g1_ref[...], shift_ref[...], scale_ref[...]).astype(BF16)

    y = jnp.dot(xn_ref[...], w_ref[...], preferred_element_type=F32)
    cos = cos_ref[...] if rope else None
    sin = sin_ref[...] if rope else None

    if want_q:
        @pl.when(j < nq)
        def _():
            q_ref[...] = _head_norm_rope(y, qg_ref[...] * ATT_SCALE, cos, sin).astype(BF16)

    @pl.when(j == nq)
    def _():
        k_ref[...] = _head_norm_rope(y, kg_ref[...], cos, sin).astype(BF16)

    @pl.when(j == nq + 1)
    def _():
        v_ref[...] = y.astype(BF16)

    if want_aux:
        @pl.when(j > nq + 1)
        def _():
            aux_ref[...] = y


def _inproj(h, mods3, row0, g1, w_in, q_g, k_g, rope_tabs, *, tm, want_q=True, want_aux=True):
    bx, lx, d = h.shape
    n_in = w_in.shape[1]
    tn = KV_WIDTH
    nq = ATT_WIDTH // tn
    n_tiles = n_in // tn
    j0 = 0 if want_q else nq
    j1 = n_tiles if want_aux else nq + 2
    rope = rope_tabs is not None
    aux_w = n_in - ATT_WIDTH - 2 * KV_WIDTH

    in_specs = [
        pl.BlockSpec((None, tm, d), lambda b, i, j: (b, i, 0)),
        pl.BlockSpec((None, 1, d), _mod_spec(row0, 0)),
        pl.BlockSpec((None, 1, d), _mod_spec(row0, 1)),
        pl.BlockSpec((1, d), lambda b, i, j: (0, 0)),
        pl.BlockSpec((d, tn), lambda b, i, j: (0, j + j0)),
        pl.BlockSpec((1, HEAD_DIM), lambda b, i, j: (0, 0)),
        pl.BlockSpec((1, HEAD_DIM), lambda b, i, j: (0, 0)),
    ]
    args = [h, mods3, mods3, g1.reshape(1, d), w_in, q_g.reshape(1, HEAD_DIM), k_g.reshape(1, HEAD_DIM)]
    if rope:
        in_specs += [pl.BlockSpec((tm, HEAD_DIM), lambda b, i, j: (i, 0))] * 2
        args += list(rope_tabs)

    out_specs, out_shape = [], []
    if want_q:
        out_specs.append(pl.BlockSpec((None, tm, tn), lambda b, i, j: (b, i, jnp.minimum(j + j0, nq - 1))))
        out_shape.append(jax.ShapeDtypeStruct((bx, lx, ATT_WIDTH), BF16))
    for _ in range(2):
        out_specs.append(pl.BlockSpec((None, tm, tn), lambda b, i, j: (b, i, 0)))
        out_shape.append(jax.ShapeDtypeStruct((bx, lx, KV_WIDTH), BF16))
    if want_aux:
        out_specs.append(pl.BlockSpec((None, tm, tn), lambda b, i, j: (b, i, jnp.maximum(j + j0 - (nq + 2), 0))))
        out_shape.append(jax.ShapeDtypeStruct((bx, lx, aux_w), F32))

    return pl.pallas_call(
        functools.partial(_inproj_kernel, j0=j0, nq=nq, rope=rope, want_q=want_q, want_aux=want_aux),
        grid=(bx, lx // tm, j1 - j0),
        in_specs=in_specs,
        out_specs=out_specs,
        out_shape=out_shape,
        scratch_shapes=[pltpu.VMEM((tm, d), BF16)],
        name="inproj",
        compiler_params=_params(("arbitrary", "arbitrary", "arbitrary")),
    )(*args)


def _softmax_chunk(qh, kch, vch, m_ref, l_ref, acc_ref, mask):
    s = lax.dot_general(qh, kch, (((1,), (1,)), ((), ())), preferred_element_type=F32)
    if mask is not None:
        s = jnp.where(mask, s, NEG_INF)
    m_prev = m_ref[...]
    m_next = jnp.maximum(m_prev, jnp.max(s, axis=1, keepdims=True))
    p = jnp.exp(s - pltpu.repeat(m_next, s.shape[1] // LANES, axis=1))
    alpha = jnp.exp(m_prev - m_next)
    l_ref[...] = alpha * l_ref[...] + jnp.sum(p, axis=1, keepdims=True)
    acc_ref[...] = alpha * acc_ref[...] + jnp.dot(p.astype(BF16), vch, preferred_element_type=F32)
    m_ref[...] = m_next


def _attn_kernel(*refs, mode, tq, tk, seq, has_sink):
    it = iter(refs)
    q_ref = next(it)
    k_ref = v_ref = None
    if mode != "ctx":
        k_ref, v_ref = next(it), next(it)
    kc_ref, vc_ref = next(it), next(it)
    sink_ref = next(it) if has_sink else None
    o_ref, m_ref, l_ref, acc_ref = next(it), next(it), next(it), next(it)

    kvh = pl.program_id(1)
    start = pl.program_id(2) * tq

    mask = None
    if mode == "window":
        ks = jnp.clip(start - WINDOW, 0, seq - tk)
        ks = pl.multiple_of(ks, WINDOW)
        rel = (lax.broadcasted_iota(jnp.int32, (tq, tk), 1) - lax.broadcasted_iota(jnp.int32, (tq, tk), 0)
               + (ks - start))
        mask = jnp.abs(rel) <= WINDOW

    for g in range(GROUPS):
        qh = q_ref[:, g * HEAD_DIM:(g + 1) * HEAD_DIM]
        m_ref[...] = jnp.full(m_ref.shape, NEG_INF, F32)
        l_ref[...] = jnp.zeros(l_ref.shape, F32)
        acc_ref[...] = jnp.zeros(acc_ref.shape, F32)
        if mode == "dense":
            for c in range(seq // tk):
                _softmax_chunk(qh, k_ref[c * tk:(c + 1) * tk, :], v_ref[c * tk:(c + 1) * tk, :],
                               m_ref, l_ref, acc_ref, None)
        elif mode == "window":
            _softmax_chunk(qh, k_ref[pl.ds(ks, tk), :], v_ref[pl.ds(ks, tk), :], m_ref, l_ref, acc_ref, mask)
        _softmax_chunk(qh, kc_ref[...], vc_ref[...], m_ref, l_ref, acc_ref, None)
        l = l_ref[...]
        if has_sink:
            l = l + jnp.exp(sink_ref[kvh * GROUPS + g] - m_ref[...])
        o_ref[:, g * HEAD_DIM:(g + 1) * HEAD_DIM] = (acc_ref[...] / l).astype(BF16)


def _attention(q, k, v, kc, vc, sink, *, mode, tq, tk):
    b, lq, _ = q.shape
    lc = kc.shape[1]
    seq = k.shape[1] if k is not None else 0
    gw = GROUPS * HEAD_DIM
    in_specs = [pl.BlockSpec((None, tq, gw), lambda bb, h, i: (bb, i, h))]
    args = [q]
    if mode != "ctx":
        in_specs += [pl.BlockSpec((None, seq, HEAD_DIM), lambda bb, h, i: (bb, 0, h))] * 2
        args += [k, v]
    in_specs += [pl.BlockSpec((None, lc, HEAD_DIM), lambda bb, h, i: (bb, 0, h))] * 2
    args += [kc, vc]
    has_sink = sink is not None
    if has_sink:
        in_specs.append(pl.BlockSpec(memory_space=pltpu.SMEM))
        args.append(sink)
    return pl.pallas_call(
        functools.partial(_attn_kernel, mode=mode, tq=tq, tk=tk, seq=seq, has_sink=has_sink),
        grid=(b, KV_HEADS, lq // tq),
        in_specs=in_specs,
        out_specs=pl.BlockSpec((None, tq, gw), lambda bb, h, i: (bb, i, h)),
        out_shape=jax.ShapeDtypeStruct((b, lq, ATT_WIDTH), BF16),
        scratch_shapes=[pltpu.VMEM((tq, LANES), F32)] * 3,
        name="attn_" + mode,
        compiler_params=_params(("arbitrary", "arbitrary", "arbitrary")),
    )(*args)


def _shift_down(x, k):
    row = lax.broadcasted_iota(jnp.int32, x.shape, 0)
    return jnp.where(row >= k, pltpu.roll(x, k, 0), 0.0)


def _shift_up(x, k):
    n = x.shape[0]
    row = lax.broadcasted_iota(jnp.int32, x.shape, 0)
    return jnp.where(row < n - k, pltpu.roll(x, n - k, 0), 0.0)


def _conv_kernel(gb_ref, gc_ref, u_ref, w_ref, o_ref):
    z = gc_ref[...] * u_ref[...]
    w = w_ref[...]
    out = w[0:1, :] * _shift_down(z, 1)
    out = out + w[1:2, :] * z
    out = out + w[2:3, :] * _shift_up(z, 1)
    o_ref[...] = (gb_ref[...] * out).astype(BF16)


def _short_conv(aux, conv_w):
    b, l, w3 = aux.shape
    w = w3 // 3
    nb = w // LANES
    blk = lambda off: pl.BlockSpec((None, l, LANES), lambda bb, c: (bb, 0, c + off))
    return pl.pallas_call(
        _conv_kernel,
        grid=(b, nb),
        in_specs=[blk(0), blk(nb), blk(2 * nb), pl.BlockSpec((3, LANES), lambda bb, c: (0, c))],
        out_specs=blk(0),
        out_shape=jax.ShapeDtypeStruct((b, l, w), BF16),
        name="short_conv",
        compiler_params=_params(("arbitrary", "arbitrary")),
    )(aux, aux, aux, conv_w)


def _pool_kernel(u_ref, w_ref, s_ref, o_ref):
    n = u_ref.shape[0]
    t = lax.broadcasted_iota(jnp.int32, (n, LANES), 0)
    for g, win in enumerate(POOL_WINDOWS):
        cols = slice(g * LANES, (g + 1) * LANES)
        u = u_ref[:, cols]
        half = win // 2
        fwd = u
        bwd = _shift_down(u, 1)
        k = 1
        while k < half:
            fwd = fwd + _shift_up(fwd, k)
            bwd = bwd + _shift_down(bwd, k)
            k *= 2
        cnt = jnp.minimum(t + half, n) - jnp.maximum(t - half, 0)
        pooled = (fwd + bwd) / cnt.astype(F32) - u
        mixed = jnp.dot(pooled.astype(BF16), w_ref[g], preferred_element_type=F32)
        o_ref[:, cols] = (mixed * s_ref[:, cols]).astype(BF16)


def _multiscale_pool(u, pool_w, pool_scale):
    b, l, w = u.shape
    return pl.pallas_call(
        _pool_kernel,
        grid=(b,),
        in_specs=[pl.BlockSpec((None, l, w), lambda bb: (bb, 0, 0)),
                  pl.BlockSpec(pool_w.shape, lambda bb: (0, 0, 0)),
                  pl.BlockSpec((1, w), lambda bb: (0, 0))],
        out_specs=pl.BlockSpec((None, l, w), lambda bb: (bb, 0, 0)),
        out_shape=jax.ShapeDtypeStruct((b, l, w), BF16),
        name="pool",
        compiler_params=_params(("arbitrary",)),
    )(u, pool_w, pool_scale.reshape(1, w))


def _outproj_kernel(a_ref, x_ref, wa_ref, wx_ref, h_ref, gate_ref, o_ref):
    y = jnp.dot(a_ref[...], wa_ref[...], preferred_element_type=F32)
    y = y + jnp.dot(x_ref[...], wx_ref[...], preferred_element_type=F32)
    o_ref[...] = h_ref[...] + gate_ref[...] * y


def _outproj(attn, auxo, w_out, h, mods3, row0, *, tm):
    bx, lx, d = h.shape
    wa, wx = attn.shape[2], auxo.shape[2]
    return pl.pallas_call(
        _outproj_kernel,
        grid=(bx, lx // tm),
        in_specs=[pl.BlockSpec((None, tm, wa), lambda b, i: (b, i, 0)),
                  pl.BlockSpec((None, tm, wx), lambda b, i: (b, i, 0)),
                  pl.BlockSpec((wa, d), lambda b, i: (0, 0)),
                  pl.BlockSpec((wx, d), lambda b, i: (wa // wx, 0)),
                  pl.BlockSpec((None, tm, d), lambda b, i: (b, i, 0)),
                  pl.BlockSpec((None, 1, d), _mod_spec(row0, 2))],
        out_specs=pl.BlockSpec((None, tm, d), lambda b, i: (b, i, 0)),
        out_shape=jax.ShapeDtypeStruct((bx, lx, d), F32),
        name="outproj",
        compiler_params=_params(("arbitrary", "arbitrary")),
    )(attn, auxo, w_out, w_out, h, mods3)


def _ffn_kernel(*refs, final):
    it = iter(refs)
    h_ref, shift_ref, scale_ref, gate_ref, g2_ref, wg_ref, wu_ref, wd_ref = (next(it) for _ in range(8))
    fg_ref = next(it) if final else None
    o_ref, hn_ref = next(it), next(it)
    j = pl.program_id(2)

    @pl.when(j == 0)
    def _():
        hn_ref[...] = _rms_modulate(h_ref[...], g2_ref[...], shift_ref[...], scale_ref[...]).astype(BF16)

    hn = hn_ref[...]
    gt = jnp.dot(hn, wg_ref[...], preferred_element_type=F32)
    up = jnp.dot(hn, wu_ref[...], preferred_element_type=F32)
    act = (gt * jax.nn.sigmoid(gt) * up).astype(BF16)
    dn = jnp.dot(act, wd_ref[...], preferred_element_type=F32)

    @pl.when(j == 0)
    def _():
        o_ref[...] = dn

    @pl.when(j > 0)
    def _():
        o_ref[...] += dn

    @pl.when(j == pl.num_programs(2) - 1)
    def _():
        out = h_ref[...] + gate_ref[...] * o_ref[...]
        if final:
            ms = jnp.mean(out * out, axis=-1, keepdims=True)
            out = out * lax.rsqrt(ms + EPS) * fg_ref[...]
        o_ref[...] = out


def _ffn(h, mods3, row0, g2, w_gate, w_up, w_down, final_g, *, tm, th):
    bx, lx, d = h.shape
    hid = w_gate.shape[1]
    final = final_g is not None
    in_specs = [pl.BlockSpec((None, tm, d), lambda b, i, j: (b, i, 0)),
                pl.BlockSpec((None, 1, d), _mod_spec(row0, 3)),
                pl.BlockSpec((None, 1, d), _mod_spec(row0, 4)),
                pl.BlockSpec((None, 1, d), _mod_spec(row0, 5)),
                pl.BlockSpec((1, d), lambda b, i, j: (0, 0)),
                pl.BlockSpec((d, th), lambda b, i, j: (0, j)),
                pl.BlockSpec((d, th), lambda b, i, j: (0, j)),
                pl.BlockSpec((th, d), lambda b, i, j: (j, 0))]
    args = [h, mods3, mods3, mods3, g2.reshape(1, d), w_gate, w_up, w_down]
    if final:
        in_specs.append(pl.BlockSpec((1, d), lambda b, i, j: (0, 0)))
        args.append(final_g.reshape(1, d))
    return pl.pallas_call(
        functools.partial(_ffn_kernel, final=final),
        grid=(bx, lx // tm, hid // th),
        in_specs=in_specs,
        out_specs=pl.BlockSpec((None, tm, d), lambda b, i, j: (b, i, 0)),
        out_shape=jax.ShapeDtypeStruct((bx, lx, d), F32),
        scratch_shapes=[pltpu.VMEM((tm, d), BF16)],
        name="ffn",
        compiler_params=_params(("arbitrary", "arbitrary", "arbitrary")),
    )(*args)


def _rope_tables(n):
    rows = n // GRID_W
    row = jnp.broadcast_to(jnp.arange(rows, dtype=F32)[:, None], (rows, GRID_W)).reshape(-1)
    col = jnp.broadcast_to(jnp.arange(GRID_W, dtype=F32)[None, :], (rows, GRID_W)).reshape(-1)
    inv = jnp.power(ROPE_THETA, -jnp.arange(0, AXIS_DIM, 2, dtype=F32) / AXIS_DIM)
    ang_r = row[:, None] * inv
    ang_c = col[:, None] * inv
    cos = jnp.concatenate([jnp.cos(ang_r), jnp.cos(ang_r), jnp.cos(ang_c), jnp.cos(ang_c)], axis=-1)
    sin = jnp.concatenate([-jnp.sin(ang_r), jnp.sin(ang_r), -jnp.sin(ang_c), jnp.sin(ang_c)], axis=-1)
    return cos, sin


def kernel(x, c, ctx, c_ctx, l0_norm1_g, l0_w_mod, l0_b_mod, l0_w_in, l0_q_norm_g, l0_k_norm_g, l0_conv_w, l0_w_out, l0_norm2_g, l0_w_gate, l0_w_up, l0_w_down, l1_norm1_g, l1_w_mod, l1_b_mod, l1_w_in, l1_q_norm_g, l1_k_norm_g, l1_sink, l1_pool_w, l1_pool_scale, l1_w_out, l1_norm2_g, l1_w_gate, l1_w_up, l1_w_down, final_norm_g):
    b, s, d = x.shape
    lc = ctx.shape[1]
    rope = _rope_tables(s)
    c_all = jnp.zeros((MOD_ROWS, d), F32).at[:b].set(c).at[b].set(c_ctx)
    ctx_row = b
    bf = lambda w: w.astype(BF16)

    mods = _modulation(c_all, l0_w_mod, l0_b_mod).reshape(MOD_ROWS, 1, N_MOD * d)
    w_in, w_out = bf(l0_w_in), bf(l0_w_out)
    w_gate, w_up, w_down = bf(l0_w_gate), bf(l0_w_up), bf(l0_w_down)
    hc = ctx.reshape(1, b * lc, d)

    q, k, v, aux = _inproj(x, mods, 0, l0_norm1_g, w_in, l0_q_norm_g, l0_k_norm_g, rope, tm=1024)
    qc, kc, vc, auxc = _inproj(hc, mods, ctx_row, l0_norm1_g, w_in, l0_q_norm_g, l0_k_norm_g, None, tm=1024)
    qc, kc, vc = (t.reshape(b, lc, -1) for t in (qc, kc, vc))

    attn = _attention(q, k, v, kc, vc, None, mode="dense", tq=512, tk=512)
    conv = _short_conv(aux, l0_conv_w)
    h = _outproj(attn, conv, w_out, x, mods, 0, tm=512)
    h = _ffn(h, mods, 0, l0_norm2_g, w_gate, w_up, w_down, None, tm=512, th=512)

    attn_c = _attention(qc, None, None, kc, vc, None, mode="ctx", tq=lc, tk=lc)
    conv_c = _short_conv(auxc.reshape(b, lc, -1), l0_conv_w)
    hc = _outproj(attn_c.reshape(1, b * lc, -1), conv_c.reshape(1, b * lc, -1), w_out, hc, mods, ctx_row, tm=512)
    hc = _ffn(hc, mods, ctx_row, l0_norm2_g, w_gate, w_up, w_down, None, tm=512, th=512)

    mods = _modulation(c_all, l1_w_mod, l1_b_mod).reshape(MOD_ROWS, 1, N_MOD * d)
    w_in, w_out = bf(l1_w_in), bf(l1_w_out)
    w_gate, w_up, w_down = bf(l1_w_gate), bf(l1_w_up), bf(l1_w_down)

    q, k, v, u = _inproj(h, mods, 0, l1_norm1_g, w_in, l1_q_norm_g, l1_k_norm_g, rope, tm=1024)
    kc, vc = _inproj(hc, mods, ctx_row, l1_norm1_g, w_in, l1_q_norm_g, l1_k_norm_g, None, tm=1024,
                     want_q=False, want_aux=False)
    kc, vc = kc.reshape(b, lc, -1), vc.reshape(b, lc, -1)

    attn = _attention(q, k, v, kc, vc, l1_sink, mode="window", tq=256, tk=512)
    pool = _multiscale_pool(u, bf(l1_pool_w), l1_pool_scale)
    h = _outproj(attn, pool, w_out, h, mods, 0, tm=512)
    return _ffn(h, mods, 0, l1_norm2_g, w_gate, w_up, w_down, final_norm_g, tm=512, th=512)
```

```python
import functools
import math

import jax
import jax.numpy as jnp
from jax import lax
from jax.experimental import pallas as pl
from jax.experimental.pallas import tpu as pltpu

F32 = jnp.float32
BF16 = jnp.bfloat16

LANES = 128
HEAD_DIM = 128
AXIS_DIM = HEAD_DIM // 2
Q_HEADS = 12
KV_HEADS = 4
GROUPS = Q_HEADS // KV_HEADS
ATT_WIDTH = Q_HEADS * HEAD_DIM
KV_WIDTH = KV_HEADS * HEAD_DIM
GRID_W = 64
WINDOW = 128
POOL_WINDOWS = (2, 4, 8, 16)
ROPE_THETA = 10000.0
N_MOD = 6
EPS = 1e-6
NEG_INF = -1e30
ATT_SCALE = HEAD_DIM ** -0.5
LOG2E = math.log2(math.e)
MOD_ROWS = 8
VMEM_LIMIT = 56 * 1024 * 1024
NORM_ROWS = 16


def _params(sem):
    return pltpu.CompilerParams(dimension_semantics=sem, vmem_limit_bytes=VMEM_LIMIT)


def _mod_spec(row0, chunk):
    def imap(b, *_):
        return (b + row0, 0, chunk)
    return imap


def _const_spec(shape):
    zeros = (0,) * len(shape)
    return pl.BlockSpec(shape, lambda *_: zeros, pipeline_mode=pl.Buffered(1))


def _mod_kernel(c_ref, w_ref, b_ref, o_ref):
    c = c_ref[...]
    sc = c * jax.nn.sigmoid(c)
    o_ref[...] = jnp.dot(sc.astype(BF16), w_ref[...].astype(BF16),
                         preferred_element_type=F32) + b_ref[...]


def _modulation(c_all, w_mod, b_mod):
    d, n = w_mod.shape
    tn = 1024
    return pl.pallas_call(
        _mod_kernel,
        grid=(n // tn,),
        in_specs=[pl.BlockSpec((MOD_ROWS, d), lambda j: (0, 0)),
                  pl.BlockSpec((d, tn), lambda j: (0, j)),
                  pl.BlockSpec((1, tn), lambda j: (0, j))],
        out_specs=pl.BlockSpec((MOD_ROWS, tn), lambda j: (0, j)),
        out_shape=jax.ShapeDtypeStruct((MOD_ROWS, n), F32),
        name="modulation",
        compiler_params=_params(("arbitrary",)),
    )(c_all, w_mod, b_mod.reshape(1, n))


def _rms_modulate_rows(x, gs, shift):
    ms = jnp.mean(x * x, axis=-1, keepdims=True)
    return x * lax.rsqrt(ms + EPS) * gs + shift


def _norm_to(dst_ref, src, gs, shift, row0=0):
    rows = src.shape[0]
    for r in range(0, rows, NORM_ROWS):
        x = src[r:r + NORM_ROWS, :]
        dst_ref[row0 + r:row0 + r + NORM_ROWS, :] = _rms_modulate_rows(x, gs, shift).astype(BF16)


def _head_norm_rope(y, g, cos, sin):
    outs = []
    for hh in range(y.shape[1] // HEAD_DIM):
        t = y[:, hh * HEAD_DIM:(hh + 1) * HEAD_DIM]
        ms = jnp.mean(t * t, axis=-1, keepdims=True)
        t = t * lax.rsqrt(ms + EPS) * g
        if cos is not None:
            t = t * cos + pltpu.roll(t, AXIS_DIM, 1) * sin
        outs.append(t)
    return jnp.concatenate(outs, axis=1)


def _inproj_kernel(*refs, tn, nq, rope, want_q, want_aux):
    it = iter(refs)
    h_ref, shift_ref, scale_ref, g1_ref, w_ref, qg_ref, kg_ref = (next(it) for _ in range(7))
    cos_ref = sin_ref = None
    if rope:
        cos_ref, sin_ref = next(it), next(it)
    q_ref = next(it) if want_q else None
    k_ref, v_ref = next(it), next(it)
    aux_ref = next(it) if want_aux else None
    xn_ref = next(it)

    _norm_to(xn_ref, h_ref, g1_ref[...] * (1.0 + scale_ref[...]), shift_ref[...])
    cos = cos_ref[...] if rope else None
    sin = sin_ref[...] if rope else None

    def tile(j):
        return jnp.dot(xn_ref[...], w_ref[:, j * tn:(j + 1) * tn], preferred_element_type=F32)

    if want_q:
        qg = qg_ref[...] * (ATT_SCALE * LOG2E)
        for j in range(nq):
            q_ref[:, j * tn:(j + 1) * tn] = _head_norm_rope(tile(j), qg, cos, sin).astype(BF16)
    k_ref[...] = _head_norm_rope(tile(nq), kg_ref[...], cos, sin).astype(BF16)
    v_ref[...] = tile(nq + 1).astype(BF16)
    if want_aux:
        for j in range(aux_ref.shape[1] // tn):
            aux_ref[:, j * tn:(j + 1) * tn] = tile(nq + 2 + j)


def _inproj(h, mods3, row0, g1, w_in, q_g, k_g, rope_tabs, *, tm, want_q=True, want_aux=True):
    bx, lx, d = h.shape
    n_in = w_in.shape[1]
    tn = KV_WIDTH
    nq = ATT_WIDTH // tn
    rope = rope_tabs is not None
    aux_w = n_in - ATT_WIDTH - 2 * KV_WIDTH
    row = lambda b, i: (b, i, 0)

    in_specs = [
        pl.BlockSpec((None, tm, d), row),
        pl.BlockSpec((None, 1, d), _mod_spec(row0, 0)),
        pl.BlockSpec((None, 1, d), _mod_spec(row0, 1)),
        _const_spec((1, d)),
        _const_spec((d, n_in)),
        _const_spec((1, HEAD_DIM)),
        _const_spec((1, HEAD_DIM)),
    ]
    args = [h, mods3, mods3, g1.reshape(1, d), w_in, q_g.reshape(1, HEAD_DIM), k_g.reshape(1, HEAD_DIM)]
    if rope:
        in_specs += [pl.BlockSpec((tm, HEAD_DIM), lambda b, i: (i, 0))] * 2
        args += list(rope_tabs)

    out_specs, out_shape = [], []
    if want_q:
        out_specs.append(pl.BlockSpec((None, tm, ATT_WIDTH), row))
        out_shape.append(jax.ShapeDtypeStruct((bx, lx, ATT_WIDTH), BF16))
    for _ in range(2):
        out_specs.append(pl.BlockSpec((None, tm, KV_WIDTH), row))
        out_shape.append(jax.ShapeDtypeStruct((bx, lx, KV_WIDTH), BF16))
    if want_aux:
        out_specs.append(pl.BlockSpec((None, tm, aux_w), row))
        out_shape.append(jax.ShapeDtypeStruct((bx, lx, aux_w), F32))

    return pl.pallas_call(
        functools.partial(_inproj_kernel, tn=tn, nq=nq, rope=rope, want_q=want_q, want_aux=want_aux),
        grid=(bx, lx // tm),
        in_specs=in_specs,
        out_specs=out_specs,
        out_shape=out_shape,
        scratch_shapes=[pltpu.VMEM((tm, d), BF16)],
        name="inproj",
        compiler_params=_params(("arbitrary", "arbitrary")),
    )(*args)


def _transpose_bf16(x):
    eye = (lax.broadcasted_iota(jnp.int32, (HEAD_DIM, HEAD_DIM), 0)
           == lax.broadcasted_iota(jnp.int32, (HEAD_DIM, HEAD_DIM), 1)).astype(BF16)
    return lax.dot_general(eye, x, (((1,), (1,)), ((), ())), preferred_element_type=F32).astype(BF16)


def _scores(kch, q, mask):
    s = lax.dot_general(kch, q, (((1,), (1,)), ((), ())), preferred_element_type=F32)
    return s if mask is None else jnp.where(mask, s, NEG_INF)


def _softmax_step(state, s, vt):
    m_prev, l_prev, acc = state
    m_next = jnp.maximum(m_prev, jnp.max(s, axis=0, keepdims=True))
    p = jnp.exp2(s - m_next)
    alpha = jnp.exp2(m_prev - m_next)
    l_next = alpha * l_prev + jnp.sum(p, axis=0, keepdims=True)
    acc = alpha * acc + jnp.dot(vt, p.astype(BF16), preferred_element_type=F32)
    return m_next, l_next, acc


def _attn_kernel(*refs, mode, tq, nsub, tk, seq, has_sink):
    it = iter(refs)
    q_ref = next(it)
    k_ref = v_ref = None
    if mode != "ctx":
        k_ref, v_ref = next(it), next(it)
    kc_ref, vc_ref = next(it), next(it)
    sink_ref = next(it) if has_sink else None
    o_ref = next(it)
    vt_ref = next(it) if mode != "ctx" else None
    vct_ref = next(it)

    kvh = pl.program_id(1)
    step = pl.program_id(2)

    @pl.when(step == 0)
    def _():
        vct_ref[...] = _transpose_bf16(vc_ref[...])
        if mode != "ctx":
            for c in range(0, seq, tk):
                vt_ref[:, c:c + tk] = _transpose_bf16(v_ref[c:c + tk, :])

    n = GROUPS * tq

    items = []
    for sub in range(nsub):
        start = (step * nsub + sub) * tq
        rows = slice(sub * tq, (sub + 1) * tq)
        q = jnp.concatenate([q_ref[rows, g * HEAD_DIM:(g + 1) * HEAD_DIM] for g in range(GROUPS)], axis=0)
        if mode == "dense":
            for c in range(0, seq, tk):
                items.append((sub, q, None, (lambda c=c: k_ref[c:c + tk, :]), (lambda c=c: vt_ref[:, c:c + tk])))
        elif mode == "window":
            ks = pl.multiple_of(jnp.clip(start - WINDOW, 0, seq - tk), WINDOW)
            qpos = start + (lax.broadcasted_iota(jnp.int32, (tk, n), 1) & (tq - 1))
            kpos = ks + lax.broadcasted_iota(jnp.int32, (tk, n), 0)
            mask = jnp.abs(kpos - qpos) <= WINDOW
            items.append((sub, q, mask, (lambda ks=ks: k_ref[pl.ds(ks, tk), :]),
                          (lambda ks=ks: vt_ref[:, pl.ds(ks, tk)])))
        items.append((sub, q, None, (lambda: kc_ref[...]), (lambda: vct_ref[...])))

    states = [(jnp.full((1, n), NEG_INF, F32), jnp.zeros((1, n), F32), jnp.zeros((HEAD_DIM, n), F32))
              for _ in range(nsub)]
    s_next = _scores(items[0][3](), items[0][1], items[0][2])
    for idx, (sub, _, _, _, vt_fn) in enumerate(items):
        s_cur = s_next
        if idx + 1 < len(items):
            nxt = items[idx + 1]
            s_next = _scores(nxt[3](), nxt[1], nxt[2])
        states[sub] = _softmax_step(states[sub], s_cur, vt_fn())

    for sub, (m, l, acc) in enumerate(states):
        if has_sink:
            sink = jnp.concatenate([jnp.full((1, tq), sink_ref[kvh * GROUPS + g] * LOG2E, F32)
                                    for g in range(GROUPS)], axis=1)
            l = l + jnp.exp2(sink - m)
        out = acc / l
        for g in range(GROUPS):
            o_ref[sub * tq:(sub + 1) * tq, g * HEAD_DIM:(g + 1) * HEAD_DIM] = (
                out[:, g * tq:(g + 1) * tq].T.astype(BF16))


def _attention(q, k, v, kc, vc, sink, *, mode, tq, nsub, tk):
    b, lq, _ = q.shape
    lc = kc.shape[1]
    seq = k.shape[1] if k is not None else 0
    gw = GROUPS * HEAD_DIM
    tstep = tq * nsub
    in_specs = [pl.BlockSpec((None, tstep, gw), lambda bb, h, i: (bb, i, h))]
    args = [q]
    scratch = []
    if mode != "ctx":
        in_specs += [pl.BlockSpec((None, seq, HEAD_DIM), lambda bb, h, i: (bb, 0, h))] * 2
        args += [k, v]
        scratch.append(pltpu.VMEM((HEAD_DIM, seq), BF16))
    in_specs += [pl.BlockSpec((None, lc, HEAD_DIM), lambda bb, h, i: (bb, 0, h))] * 2
    args += [kc, vc]
    scratch.append(pltpu.VMEM((HEAD_DIM, lc), BF16))
    has_sink = sink is not None
    if has_sink:
        in_specs.append(pl.BlockSpec(memory_space=pltpu.SMEM))
        args.append(sink)
    return pl.pallas_call(
        functools.partial(_attn_kernel, mode=mode, tq=tq, nsub=nsub, tk=tk, seq=seq, has_sink=has_sink),
        grid=(b, KV_HEADS, lq // tstep),
        in_specs=in_specs,
        out_specs=pl.BlockSpec((None, tstep, gw), lambda bb, h, i: (bb, i, h)),
        out_shape=jax.ShapeDtypeStruct((b, lq, ATT_WIDTH), BF16),
        scratch_shapes=scratch,
        name="attn_" + mode,
        compiler_params=_params(("arbitrary", "arbitrary", "arbitrary")),
    )(*args)


def _shift_down(x, k):
    row = lax.broadcasted_iota(jnp.int32, x.shape, 0)
    return jnp.where(row >= k, pltpu.roll(x, k, 0), 0.0)


def _shift_up(x, k):
    n = x.shape[0]
    row = lax.broadcasted_iota(jnp.int32, x.shape, 0)
    return jnp.where(row < n - k, pltpu.roll(x, n - k, 0), 0.0)


def _conv_kernel(gb_ref, gc_ref, u_ref, w_ref, o_ref):
    z = gc_ref[...] * u_ref[...]
    w = w_ref[...]
    out = w[0:1, :] * _shift_down(z, 1)
    out = out + w[1:2, :] * z
    out = out + w[2:3, :] * _shift_up(z, 1)
    o_ref[...] = (gb_ref[...] * out).astype(BF16)


def _short_conv(aux, conv_w):
    b, l, w3 = aux.shape
    w = w3 // 3
    nb = w // LANES
    blk = lambda off: pl.BlockSpec((None, l, LANES), lambda bb, c: (bb, 0, c + off))
    return pl.pallas_call(
        _conv_kernel,
        grid=(b, nb),
        in_specs=[blk(0), blk(nb), blk(2 * nb), pl.BlockSpec((3, LANES), lambda bb, c: (0, c))],
        out_specs=blk(0),
        out_shape=jax.ShapeDtypeStruct((b, l, w), BF16),
        name="short_conv",
        compiler_params=_params(("arbitrary", "arbitrary")),
    )(aux, aux, aux, conv_w)


def _pool_kernel(u_ref, w_ref, s_ref, o_ref):
    n = u_ref.shape[0]
    t = lax.broadcasted_iota(jnp.int32, (n, LANES), 0)
    for g, win in enumerate(POOL_WINDOWS):
        cols = slice(g * LANES, (g + 1) * LANES)
        u = u_ref[:, cols]
        half = win // 2
        fwd = u
        bwd = _shift_down(u, 1)
        k = 1
        while k < half:
            fwd = fwd + _shift_up(fwd, k)
            bwd = bwd + _shift_down(bwd, k)
            k *= 2
        cnt = jnp.minimum(t + half, n) - jnp.maximum(t - half, 0)
        pooled = (fwd + bwd) / cnt.astype(F32) - u
        mixed = jnp.dot(pooled.astype(BF16), w_ref[g], preferred_element_type=F32)
        o_ref[:, cols] = (mixed * s_ref[:, cols]).astype(BF16)


def _multiscale_pool(u, pool_w, pool_scale):
    b, l, w = u.shape
    return pl.pallas_call(
        _pool_kernel,
        grid=(b,),
        in_specs=[pl.BlockSpec((None, l, w), lambda bb: (bb, 0, 0)),
                  pl.BlockSpec(pool_w.shape, lambda bb: (0, 0, 0)),
                  pl.BlockSpec((1, w), lambda bb: (0, 0))],
        out_specs=pl.BlockSpec((None, l, w), lambda bb: (bb, 0, 0)),
        out_shape=jax.ShapeDtypeStruct((b, l, w), BF16),
        name="pool",
        compiler_params=_params(("arbitrary",)),
    )(u, pool_w, pool_scale.reshape(1, w))


def _outproj_kernel(a_ref, x_ref, wa_ref, wx_ref, h_ref, gate_ref, shift_ref, scale_ref, g2_ref,
                    o_ref, hn_ref, *, rows):
    gate = gate_ref[...]
    gs = g2_ref[...] * (1.0 + scale_ref[...])
    shift = shift_ref[...]
    for r in range(0, a_ref.shape[0], rows):
        rs = slice(r, r + rows)
        y = jnp.dot(a_ref[rs, :], wa_ref[...], preferred_element_type=F32)
        y = y + jnp.dot(x_ref[rs, :], wx_ref[...], preferred_element_type=F32)
        o_ref[rs, :] = h_ref[rs, :] + gate * y
        _norm_to(hn_ref, o_ref.at[rs, :], gs, shift, row0=r)


def _outproj(attn, auxo, w_out, h, mods3, row0, g2, *, tm):
    bx, lx, d = h.shape
    wa, wx = attn.shape[2], auxo.shape[2]
    row = lambda b, i: (b, i, 0)
    return pl.pallas_call(
        functools.partial(_outproj_kernel, rows=tm // 2),
        grid=(bx, lx // tm),
        in_specs=[pl.BlockSpec((None, tm, wa), row),
                  pl.BlockSpec((None, tm, wx), row),
                  pl.BlockSpec((wa, d), lambda b, i: (0, 0), pipeline_mode=pl.Buffered(1)),
                  pl.BlockSpec((wx, d), lambda b, i: (wa // wx, 0), pipeline_mode=pl.Buffered(1)),
                  pl.BlockSpec((None, tm, d), row),
                  pl.BlockSpec((None, 1, d), _mod_spec(row0, 2)),
                  pl.BlockSpec((None, 1, d), _mod_spec(row0, 3)),
                  pl.BlockSpec((None, 1, d), _mod_spec(row0, 4)),
                  _const_spec((1, d))],
        out_specs=[pl.BlockSpec((None, tm, d), row), pl.BlockSpec((None, tm, d), row)],
        out_shape=[jax.ShapeDtypeStruct((bx, lx, d), F32), jax.ShapeDtypeStruct((bx, lx, d), BF16)],
        name="outproj",
        compiler_params=_params(("arbitrary", "arbitrary")),
    )(attn, auxo, w_out, w_out, h, mods3, mods3, mods3, g2.reshape(1, d))


def _ffn_kernel(*refs, final):
    it = iter(refs)
    hn_ref, h_ref, gate_ref, wg_ref, wu_ref, wd_ref = (next(it) for _ in range(6))
    fg_ref = next(it) if final else None
    o_ref = next(it)
    j = pl.program_id(2)

    @pl.when(j == 0)
    def _():
        o_ref[...] = h_ref[...]

    hn = hn_ref[...]
    gt = jnp.dot(hn, wg_ref[...], preferred_element_type=F32)
    up = jnp.dot(hn, wu_ref[...], preferred_element_type=F32)
    act = (gt * jax.nn.sigmoid(gt) * up).astype(BF16)
    o_ref[...] += gate_ref[...] * jnp.dot(act, wd_ref[...], preferred_element_type=F32)

    if final:
        @pl.when(j == pl.num_programs(2) - 1)
        def _():
            fg = fg_ref[...]
            for r in range(0, o_ref.shape[0], NORM_ROWS):
                x = o_ref[r:r + NORM_ROWS, :]
                ms = jnp.mean(x * x, axis=-1, keepdims=True)
                o_ref[r:r + NORM_ROWS, :] = x * lax.rsqrt(ms + EPS) * fg


def _ffn(hn, h, mods3, row0, w_gate, w_up, w_down, final_g, *, tm, th):
    bx, lx, d = h.shape
    hid = w_gate.shape[1]
    final = final_g is not None
    row = lambda b, i, j: (b, i, 0)
    in_specs = [pl.BlockSpec((None, tm, d), row),
                pl.BlockSpec((None, tm, d), row),
                pl.BlockSpec((None, 1, d), _mod_spec(row0, 5)),
                pl.BlockSpec((d, th), lambda b, i, j: (0, j)),
                pl.BlockSpec((d, th), lambda b, i, j: (0, j)),
                pl.BlockSpec((th, d), lambda b, i, j: (j, 0))]
    args = [hn, h, mods3, w_gate, w_up, w_down]
    if final:
        in_specs.append(_const_spec((1, d)))
        args.append(final_g.reshape(1, d))
    return pl.pallas_call(
        functools.partial(_ffn_kernel, final=final),
        grid=(bx, lx // tm, hid // th),
        in_specs=in_specs,
        out_specs=pl.BlockSpec((None, tm, d), row),
        out_shape=jax.ShapeDtypeStruct((bx, lx, d), F32),
        name="ffn",
        compiler_params=_params(("arbitrary", "arbitrary", "arbitrary")),
    )(*args)


def _rope_tables(n):
    rows = n // GRID_W
    row = jnp.broadcast_to(jnp.arange(rows, dtype=F32)[:, None], (rows, GRID_W)).reshape(-1)
    col = jnp.broadcast_to(jnp.arange(GRID_W, dtype=F32)[None, :], (rows, GRID_W)).reshape(-1)
    inv = jnp.power(ROPE_THETA, -jnp.arange(0, AXIS_DIM, 2, dtype=F32) / AXIS_DIM)
    ang_r = row[:, None] * inv
    ang_c = col[:, None] * inv
    cos = jnp.concatenate([jnp.cos(ang_r), jnp.cos(ang_c), jnp.cos(ang_r), jnp.cos(ang_c)], axis=-1)
    sin = jnp.concatenate([-jnp.sin(ang_r), -jnp.sin(ang_c), jnp.sin(ang_r), jnp.sin(ang_c)], axis=-1)
    return cos, sin


def _swap_mid(t, axis):
    parts = jnp.split(t, 4, axis=axis)
    return jnp.concatenate([parts[0], parts[2], parts[1], parts[3]], axis=axis)


def _head_layout_w(w_in):
    d = w_in.shape[0]
    nqk = ATT_WIDTH + KV_WIDTH
    qk = w_in[:, :nqk].reshape(d, nqk // HEAD_DIM, 4, HEAD_DIM // 4)
    qk = _swap_mid(qk, 2).reshape(d, nqk)
    return jnp.concatenate([qk, w_in[:, nqk:]], axis=1).astype(BF16)


def _head_layout_g(g):
    return _swap_mid(g.reshape(4, HEAD_DIM // 4), 0).reshape(HEAD_DIM)


def kernel(x, c, ctx, c_ctx, l0_norm1_g, l0_w_mod, l0_b_mod, l0_w_in, l0_q_norm_g, l0_k_norm_g, l0_conv_w, l0_w_out, l0_norm2_g, l0_w_gate, l0_w_up, l0_w_down, l1_norm1_g, l1_w_mod, l1_b_mod, l1_w_in, l1_q_norm_g, l1_k_norm_g, l1_sink, l1_pool_w, l1_pool_scale, l1_w_out, l1_norm2_g, l1_w_gate, l1_w_up, l1_w_down, final_norm_g):
    b, s, d = x.shape
    lc = ctx.shape[1]
    rope = _rope_tables(s)
    c_all = jnp.zeros((MOD_ROWS, d), F32).at[:b].set(c).at[b].set(c_ctx)
    ctx_row = b
    bf = lambda w: w.astype(BF16)
    tm_in, tm_out, tm_ffn, th = 512, 512, 512, 512

    mods = _modulation(c_all, l0_w_mod, l0_b_mod).reshape(MOD_ROWS, 1, N_MOD * d)
    w_in, w_out = _head_layout_w(l0_w_in), bf(l0_w_out)
    q_g, k_g = _head_layout_g(l0_q_norm_g), _head_layout_g(l0_k_norm_g)
    w_gate, w_up, w_down = bf(l0_w_gate), bf(l0_w_up), bf(l0_w_down)
    hc = ctx.reshape(1, b * lc, d)

    q, k, v, aux = _inproj(x, mods, 0, l0_norm1_g, w_in, q_g, k_g, rope, tm=tm_in)
    qc, kc, vc, auxc = _inproj(hc, mods, ctx_row, l0_norm1_g, w_in, q_g, k_g, None, tm=tm_in)
    qc, kc, vc = (t.reshape(b, lc, -1) for t in (qc, kc, vc))

    attn = _attention(q, k, v, kc, vc, None, mode="dense", tq=512, nsub=1, tk=512)
    conv = _short_conv(aux, l0_conv_w)
    h, hn = _outproj(attn, conv, w_out, x, mods, 0, l0_norm2_g, tm=tm_out)
    h = _ffn(hn, h, mods, 0, w_gate, w_up, w_down, None, tm=tm_ffn, th=th)

    attn_c = _attention(qc, None, None, kc, vc, None, mode="ctx", tq=lc, nsub=1, tk=lc)
    conv_c = _short_conv(auxc.reshape(b, lc, -1), l0_conv_w)
    hc, hcn = _outproj(attn_c.reshape(1, b * lc, -1), conv_c.reshape(1, b * lc, -1), w_out, hc, mods, ctx_row,
                       l0_norm2_g, tm=tm_out)
    hc = _ffn(hcn, hc, mods, ctx_row, w_gate, w_up, w_down, None, tm=tm_ffn, th=th)

    mods = _modulation(c_all, l1_w_mod, l1_b_mod).reshape(MOD_ROWS, 1, N_MOD * d)
    w_in, w_out = _head_layout_w(l1_w_in), bf(l1_w_out)
    q_g, k_g = _head_layout_g(l1_q_norm_g), _head_layout_g(l1_k_norm_g)
    w_gate, w_up, w_down = bf(l1_w_gate), bf(l1_w_up), bf(l1_w_down)

    q, k, v, u = _inproj(h, mods, 0, l1_norm1_g, w_in, q_g, k_g, rope, tm=tm_in)
    kc, vc = _inproj(hc, mods, ctx_row, l1_norm1_g, w_in, q_g, k_g, None, tm=tm_in,
                     want_q=False, want_aux=False)
    kc, vc = kc.reshape(b, lc, -1), vc.reshape(b, lc, -1)

    attn = _attention(q, k, v, kc, vc, l1_sink, mode="window", tq=256, nsub=2, tk=512)
    pool = _multiscale_pool(u, bf(l1_pool_w), l1_pool_scale)
    h, hn = _outproj(attn, pool, w_out, h, mods, 0, l1_norm2_g, tm=tm_out)
    return _ffn(hn, h, mods, 0, w_gate, w_up, w_down, final_norm_g, tm=tm_ffn, th=th)
```

```python
import functools
import math

import jax
import jax.numpy as jnp
from jax import lax
from jax.experimental import pallas as pl
from jax.experimental.pallas import tpu as pltpu

F32 = jnp.float32
BF16 = jnp.bfloat16

LANES = 128
HEAD_DIM = 128
AXIS_DIM = HEAD_DIM // 2
Q_HEADS = 12
KV_HEADS = 4
GROUPS = Q_HEADS // KV_HEADS
ATT_WIDTH = Q_HEADS * HEAD_DIM
KV_WIDTH = KV_HEADS * HEAD_DIM
GRID_W = 64
WINDOW = 128
POOL_WINDOWS = (2, 4, 8, 16)
ROPE_THETA = 10000.0
N_MOD = 6
EPS = 1e-6
NEG_INF = -1e30
ATT_SCALE = HEAD_DIM ** -0.5
LOG2E = math.log2(math.e)
MOD_ROWS = 8
VMEM_LIMIT = 56 * 1024 * 1024
NORM_ROWS = 16
SCORE_LOOKAHEAD = 2
VT_ROWS = HEAD_DIM + 16


def _params(sem):
    return pltpu.CompilerParams(dimension_semantics=sem, vmem_limit_bytes=VMEM_LIMIT)


def _mod_spec(row0, chunk):
    def imap(b, *_):
        return (b + row0, 0, chunk)
    return imap


def _const_spec(shape):
    zeros = (0,) * len(shape)
    return pl.BlockSpec(shape, lambda *_: zeros, pipeline_mode=pl.Buffered(1))


def _mod_kernel(c_ref, w_ref, b_ref, o_ref):
    c = c_ref[...]
    sc = c * jax.nn.sigmoid(c)
    o_ref[...] = jnp.dot(sc.astype(BF16), w_ref[...].astype(BF16),
                         preferred_element_type=F32) + b_ref[...]


def _modulation(c_all, w_mod, b_mod):
    d, n = w_mod.shape
    tn = 1024
    return pl.pallas_call(
        _mod_kernel,
        grid=(n // tn,),
        in_specs=[pl.BlockSpec((MOD_ROWS, d), lambda j: (0, 0)),
                  pl.BlockSpec((d, tn), lambda j: (0, j)),
                  pl.BlockSpec((1, tn), lambda j: (0, j))],
        out_specs=pl.BlockSpec((MOD_ROWS, tn), lambda j: (0, j)),
        out_shape=jax.ShapeDtypeStruct((MOD_ROWS, n), F32),
        name="modulation",
        compiler_params=_params(("arbitrary",)),
    )(c_all, w_mod, b_mod.reshape(1, n))


def _rms_modulate_rows(x, gs, shift):
    ms = jnp.mean(x * x, axis=-1, keepdims=True)
    return x * lax.rsqrt(ms + EPS) * gs + shift


def _norm_to(dst_ref, src, gs, shift, row0=0):
    rows = src.shape[0]
    for r in range(0, rows, NORM_ROWS):
        x = src[r:r + NORM_ROWS, :]
        dst_ref[row0 + r:row0 + r + NORM_ROWS, :] = _rms_modulate_rows(x, gs, shift).astype(BF16)


def _head_norm_rope(y, g, cos, sin):
    outs = []
    for hh in range(y.shape[1] // HEAD_DIM):
        t = y[:, hh * HEAD_DIM:(hh + 1) * HEAD_DIM]
        ms = jnp.mean(t * t, axis=-1, keepdims=True)
        t = t * lax.rsqrt(ms + EPS) * g
        if cos is not None:
            t = t * cos + pltpu.roll(t, AXIS_DIM, 1) * sin
        outs.append(t)
    return jnp.concatenate(outs, axis=1)


def _inproj_kernel(*refs, tn, nq, rope, want_q, want_aux):
    it = iter(refs)
    h_ref, shift_ref, scale_ref, g1_ref, wqk_ref, wr_ref, qg_ref, kg_ref = (next(it) for _ in range(8))
    cos_ref = sin_ref = None
    if rope:
        cos_ref, sin_ref = next(it), next(it)
    q_ref = next(it) if want_q else None
    k_ref, v_ref = next(it), next(it)
    aux_ref = next(it) if want_aux else None
    xn_ref = next(it)

    _norm_to(xn_ref, h_ref, g1_ref[...] * (1.0 + scale_ref[...]), shift_ref[...])
    cos = cos_ref[...] if rope else None
    sin = sin_ref[...] if rope else None

    def tile(w_ref, j):
        return jnp.dot(xn_ref[...], w_ref[:, j * tn:(j + 1) * tn], preferred_element_type=F32)

    if want_q:
        qg = qg_ref[...] * (ATT_SCALE * LOG2E)
        for j in range(nq):
            q_ref[:, j * tn:(j + 1) * tn] = _head_norm_rope(tile(wqk_ref, j), qg, cos, sin).astype(BF16)
    k_ref[...] = _head_norm_rope(tile(wqk_ref, nq), kg_ref[...], cos, sin).astype(BF16)
    v_ref[...] = tile(wr_ref, 0).astype(BF16)
    if want_aux:
        for j in range(aux_ref.shape[1] // tn):
            aux_ref[:, j * tn:(j + 1) * tn] = tile(wr_ref, 1 + j)


def _inproj(h, mods3, row0, g1, w_qk, w_rest, q_g, k_g, rope_tabs, *, tm, want_q=True, want_aux=True):
    bx, lx, d = h.shape
    tn = KV_WIDTH
    nq = ATT_WIDTH // tn
    rope = rope_tabs is not None
    aux_w = w_rest.shape[1] - KV_WIDTH
    row = lambda b, i: (b, i, 0)

    in_specs = [
        pl.BlockSpec((None, tm, d), row),
        pl.BlockSpec((None, 1, d), _mod_spec(row0, 0)),
        pl.BlockSpec((None, 1, d), _mod_spec(row0, 1)),
        _const_spec((1, d)),
        _const_spec(w_qk.shape),
        _const_spec(w_rest.shape),
        _const_spec((1, HEAD_DIM)),
        _const_spec((1, HEAD_DIM)),
    ]
    args = [h, mods3, mods3, g1.reshape(1, d), w_qk, w_rest, q_g.reshape(1, HEAD_DIM), k_g.reshape(1, HEAD_DIM)]
    if rope:
        in_specs += [pl.BlockSpec((tm, HEAD_DIM), lambda b, i: (i, 0))] * 2
        args += list(rope_tabs)

    out_specs, out_shape = [], []
    if want_q:
        out_specs.append(pl.BlockSpec((None, tm, ATT_WIDTH), row))
        out_shape.append(jax.ShapeDtypeStruct((bx, lx, ATT_WIDTH), BF16))
    for _ in range(2):
        out_specs.append(pl.BlockSpec((None, tm, KV_WIDTH), row))
        out_shape.append(jax.ShapeDtypeStruct((bx, lx, KV_WIDTH), BF16))
    if want_aux:
        out_specs.append(pl.BlockSpec((None, tm, aux_w), row))
        out_shape.append(jax.ShapeDtypeStruct((bx, lx, aux_w), F32))

    return pl.pallas_call(
        functools.partial(_inproj_kernel, tn=tn, nq=nq, rope=rope, want_q=want_q, want_aux=want_aux),
        grid=(bx, lx // tm),
        in_specs=in_specs,
        out_specs=out_specs,
        out_shape=out_shape,
        scratch_shapes=[pltpu.VMEM((tm, d), BF16)],
        name="inproj",
        compiler_params=_params(("arbitrary", "arbitrary")),
    )(*args)


def _transpose_bf16(x):
    eye = (lax.broadcasted_iota(jnp.int32, (HEAD_DIM, HEAD_DIM), 0)
           == lax.broadcasted_iota(jnp.int32, (HEAD_DIM, HEAD_DIM), 1)).astype(BF16)
    return lax.dot_general(eye, x, (((1,), (1,)), ((), ())), preferred_element_type=F32).astype(BF16)


def _scores(kch, q, mask):
    s = lax.dot_general(kch, q, (((1,), (1,)), ((), ())), preferred_element_type=F32)
    return s if mask is None else jnp.where(mask, s, NEG_INF)


def _softmax_step(state, s, vt):
    m_prev, acc = state
    m_next = jnp.maximum(m_prev, jnp.max(s, axis=0, keepdims=True))
    p = jnp.exp2(s - m_next).astype(BF16)
    alpha = jnp.exp2(m_prev - m_next)
    acc = alpha * acc + jnp.dot(vt, p, preferred_element_type=F32)
    return m_next, acc


def _attn_kernel(*refs, mode, tq, nsub, tk, seq, has_sink):
    it = iter(refs)
    q_ref = next(it)
    k_ref = v_ref = None
    if mode != "ctx":
        k_ref, v_ref = next(it), next(it)
    kc_ref, vc_ref = next(it), next(it)
    sink_ref = next(it) if has_sink else None
    o_ref = next(it)
    vt_ref = next(it) if mode != "ctx" else None
    vct_ref = next(it)

    kvh = pl.program_id(1)
    step = pl.program_id(2)

    @pl.when(step == 0)
    def _():
        vct_ref[:HEAD_DIM, :] = _transpose_bf16(vc_ref[...])
        vct_ref[HEAD_DIM:, :] = jnp.ones((VT_ROWS - HEAD_DIM, vct_ref.shape[1]), BF16)
        if mode != "ctx":
            vt_ref[HEAD_DIM:, :] = jnp.ones((VT_ROWS - HEAD_DIM, seq), BF16)
            for c in range(0, seq, tk):
                vt_ref[:HEAD_DIM, c:c + tk] = _transpose_bf16(v_ref[c:c + tk, :])

    n = GROUPS * tq
    if mode == "window":
        band0 = (lax.broadcasted_iota(jnp.int32, (tk, n), 0)
                 - (lax.broadcasted_iota(jnp.int32, (tk, n), 1) & (tq - 1)) + WINDOW)

    items = []
    for sub in range(nsub):
        start = (step * nsub + sub) * tq
        rows = slice(sub * tq, (sub + 1) * tq)
        q = jnp.concatenate([q_ref[rows, g * HEAD_DIM:(g + 1) * HEAD_DIM] for g in range(GROUPS)], axis=0)
        if mode == "dense":
            for c in range(0, seq, tk):
                items.append((sub, q, None, (lambda c=c: k_ref[c:c + tk, :]), (lambda c=c: vt_ref[:, c:c + tk])))
        elif mode == "window":
            ks = pl.multiple_of(jnp.clip(start - WINDOW, 0, seq - tk), WINDOW)
            mask = (band0 + (ks - start)).astype(jnp.uint32) <= 2 * WINDOW
            items.append((sub, q, mask, (lambda ks=ks: k_ref[pl.ds(ks, tk), :]),
                          (lambda ks=ks: vt_ref[:, pl.ds(ks, tk)])))
        items.append((sub, q, None, (lambda: kc_ref[...]), (lambda: vct_ref[...])))

    per_sub = len(items) // nsub
    items = [items[sub * per_sub + c] for c in range(per_sub) for sub in range(nsub)]

    states = [(jnp.full((1, n), NEG_INF, F32), jnp.zeros((VT_ROWS, n), F32)) for _ in range(nsub)]
    score = lambda item: _scores(item[3](), item[1], item[2])
    pending = [score(item) for item in items[:SCORE_LOOKAHEAD]]
    for idx, (sub, _, _, _, vt_fn) in enumerate(items):
        s_cur = pending.pop(0)
        if idx + SCORE_LOOKAHEAD < len(items):
            pending.append(score(items[idx + SCORE_LOOKAHEAD]))
        states[sub] = _softmax_step(states[sub], s_cur, vt_fn())

    for sub, (m, acc) in enumerate(states):
        l = acc[HEAD_DIM:HEAD_DIM + 1, :]
        acc = acc[:HEAD_DIM, :]
        if has_sink:
            sink = jnp.concatenate([jnp.full((1, tq), sink_ref[kvh * GROUPS + g] * LOG2E, F32)
                                    for g in range(GROUPS)], axis=1)
            l = l + jnp.exp2(sink - m)
        out = acc / l
        for g in range(GROUPS):
            o_ref[sub * tq:(sub + 1) * tq, g * HEAD_DIM:(g + 1) * HEAD_DIM] = (
                out[:, g * tq:(g + 1) * tq].T.astype(BF16))


def _attention(q, k, v, kc, vc, sink, *, mode, tq, nsub, tk):
    b, lq, _ = q.shape
    lc = kc.shape[1]
    seq = k.shape[1] if k is not None else 0
    gw = GROUPS * HEAD_DIM
    tstep = tq * nsub
    in_specs = [pl.BlockSpec((None, tstep, gw), lambda bb, h, i: (bb, i, h))]
    args = [q]
    scratch = []
    if mode != "ctx":
        in_specs += [pl.BlockSpec((None, seq, HEAD_DIM), lambda bb, h, i: (bb, 0, h))] * 2
        args += [k, v]
        scratch.append(pltpu.VMEM((VT_ROWS, seq), BF16))
    in_specs += [pl.BlockSpec((None, lc, HEAD_DIM), lambda bb, h, i: (bb, 0, h))] * 2
    args += [kc, vc]
    scratch.append(pltpu.VMEM((VT_ROWS, lc), BF16))
    has_sink = sink is not None
    if has_sink:
        in_specs.append(pl.BlockSpec(memory_space=pltpu.SMEM))
        args.append(sink)
    return pl.pallas_call(
        functools.partial(_attn_kernel, mode=mode, tq=tq, nsub=nsub, tk=tk, seq=seq, has_sink=has_sink),
        grid=(b, KV_HEADS, lq // tstep),
        in_specs=in_specs,
        out_specs=pl.BlockSpec((None, tstep, gw), lambda bb, h, i: (bb, i, h)),
        out_shape=jax.ShapeDtypeStruct((b, lq, ATT_WIDTH), BF16),
        scratch_shapes=scratch,
        name="attn_" + mode,
        compiler_params=_params(("arbitrary", "arbitrary", "arbitrary")),
    )(*args)


def _shift_down(x, k):
    row = lax.broadcasted_iota(jnp.int32, x.shape, 0)
    return jnp.where(row >= k, pltpu.roll(x, k, 0), 0.0)


def _shift_up(x, k):
    n = x.shape[0]
    row = lax.broadcasted_iota(jnp.int32, x.shape, 0)
    return jnp.where(row < n - k, pltpu.roll(x, n - k, 0), 0.0)


def _conv_kernel(gb_ref, gc_ref, u_ref, w_ref, o_ref):
    z = gc_ref[...] * u_ref[...]
    w = w_ref[...]
    out = w[0:1, :] * _shift_down(z, 1)
    out = out + w[1:2, :] * z
    out = out + w[2:3, :] * _shift_up(z, 1)
    o_ref[...] = (gb_ref[...] * out).astype(BF16)


def _short_conv(aux, conv_w):
    b, l, w3 = aux.shape
    w = w3 // 3
    nb = w // LANES
    blk = lambda off: pl.BlockSpec((None, l, LANES), lambda bb, c: (bb, 0, c + off))
    return pl.pallas_call(
        _conv_kernel,
        grid=(b, nb),
        in_specs=[blk(0), blk(nb), blk(2 * nb), pl.BlockSpec((3, LANES), lambda bb, c: (0, c))],
        out_specs=blk(0),
        out_shape=jax.ShapeDtypeStruct((b, l, w), BF16),
        name="short_conv",
        compiler_params=_params(("arbitrary", "arbitrary")),
    )(aux, aux, aux, conv_w)


def _pool_kernel(u_ref, w_ref, s_ref, o_ref):
    n = u_ref.shape[0]
    t = lax.broadcasted_iota(jnp.int32, (n, LANES), 0)
    for g, win in enumerate(POOL_WINDOWS):
        cols = slice(g * LANES, (g + 1) * LANES)
        u = u_ref[:, cols]
        half = win // 2
        fwd = u
        bwd = _shift_down(u, 1)
        k = 1
        while k < half:
            fwd = fwd + _shift_up(fwd, k)
            bwd = bwd + _shift_down(bwd, k)
            k *= 2
        cnt = jnp.minimum(t + half, n) - jnp.maximum(t - half, 0)
        pooled = (fwd + bwd) / cnt.astype(F32) - u
        mixed = jnp.dot(pooled.astype(BF16), w_ref[g], preferred_element_type=F32)
        o_ref[:, cols] = (mixed * s_ref[:, cols]).astype(BF16)


def _multiscale_pool(u, pool_w, pool_scale):
    b, l, w = u.shape
    return pl.pallas_call(
        _pool_kernel,
        grid=(b,),
        in_specs=[pl.BlockSpec((None, l, w), lambda bb: (bb, 0, 0)),
                  pl.BlockSpec(pool_w.shape, lambda bb: (0, 0, 0)),
                  pl.BlockSpec((1, w), lambda bb: (0, 0))],
        out_specs=pl.BlockSpec((None, l, w), lambda bb: (bb, 0, 0)),
        out_shape=jax.ShapeDtypeStruct((b, l, w), BF16),
        name="pool",
        compiler_params=_params(("arbitrary",)),
    )(u, pool_w, pool_scale.reshape(1, w))


def _outproj_kernel(a_ref, x_ref, wa_ref, wx_ref, h_ref, gate_ref, shift_ref, scale_ref, g2_ref,
                    o_ref, hn_ref, *, rows):
    gate = gate_ref[...]
    gs = g2_ref[...] * (1.0 + scale_ref[...])
    shift = shift_ref[...]
    for r in range(0, a_ref.shape[0], rows):
        rs = slice(r, r + rows)
        y = jnp.dot(a_ref[rs, :], wa_ref[...], preferred_element_type=F32)
        y = y + jnp.dot(x_ref[rs, :], wx_ref[...], preferred_element_type=F32)
        o_ref[rs, :] = h_ref[rs, :] + gate * y
        _norm_to(hn_ref, o_ref.at[rs, :], gs, shift, row0=r)


def _outproj(attn, auxo, w_out, h, mods3, row0, g2, *, tm):
    bx, lx, d = h.shape
    wa, wx = attn.shape[2], auxo.shape[2]
    row = lambda b, i: (b, i, 0)
    return pl.pallas_call(
        functools.partial(_outproj_kernel, rows=tm // 2),
        grid=(bx, lx // tm),
        in_specs=[pl.BlockSpec((None, tm, wa), row),
                  pl.BlockSpec((None, tm, wx), row),
                  pl.BlockSpec((wa, d), lambda b, i: (0, 0), pipeline_mode=pl.Buffered(1)),
                  pl.BlockSpec((wx, d), lambda b, i: (wa // wx, 0), pipeline_mode=pl.Buffered(1)),
                  pl.BlockSpec((None, tm, d), row),
                  pl.BlockSpec((None, 1, d), _mod_spec(row0, 2)),
                  pl.BlockSpec((None, 1, d), _mod_spec(row0, 3)),
                  pl.BlockSpec((None, 1, d), _mod_spec(row0, 4)),
                  _const_spec((1, d))],
        out_specs=[pl.BlockSpec((None, tm, d), row), pl.BlockSpec((None, tm, d), row)],
        out_shape=[jax.ShapeDtypeStruct((bx, lx, d), F32), jax.ShapeDtypeStruct((bx, lx, d), BF16)],
        name="outproj",
        compiler_params=_params(("arbitrary", "arbitrary")),
    )(attn, auxo, w_out, w_out, h, mods3, mods3, mods3, g2.reshape(1, d))


def _ffn_kernel(*refs, final, rows, ncols):
    it = iter(refs)
    hn_ref, h_ref, gate_ref, wg_ref, wu_ref, wd_ref = (next(it) for _ in range(6))
    fg_ref = next(it) if final else None
    o_ref = next(it)
    j = pl.program_id(2)
    tm, d = o_ref.shape
    hcols = h_ref.shape[1]
    n_hblk = d // hcols

    @pl.when(j == 0)
    def _():
        o_ref[...] = jnp.zeros(o_ref.shape, F32)

    col = pl.multiple_of(jnp.minimum(j, n_hblk - 1) * hcols, hcols)
    o_ref[:, pl.ds(col, hcols)] += jnp.where(j < n_hblk, h_ref[...], 0.0)

    gate = gate_ref[...]
    for r in range(0, tm, rows):
        rs = slice(r, r + rows)
        hn = hn_ref[rs, :]
        gt = jnp.dot(hn, wg_ref[...], preferred_element_type=F32)
        up = jnp.dot(hn, wu_ref[...], preferred_element_type=F32)
        act = (gt * jax.nn.sigmoid(gt) * up).astype(BF16)
        for c in range(0, d, ncols):
            cs = slice(c, c + ncols)
            o_ref[rs, cs] += gate[:, cs] * jnp.dot(act, wd_ref[:, cs], preferred_element_type=F32)

    if final:
        @pl.when(j == pl.num_programs(2) - 1)
        def _():
            fg = fg_ref[...]
            for r in range(0, o_ref.shape[0], NORM_ROWS):
                x = o_ref[r:r + NORM_ROWS, :]
                ms = jnp.mean(x * x, axis=-1, keepdims=True)
                o_ref[r:r + NORM_ROWS, :] = x * lax.rsqrt(ms + EPS) * fg


def _ffn(hn, h, mods3, row0, w_gate, w_up, w_down, final_g, *, tm, th, rows, ncols, hcols):
    bx, lx, d = h.shape
    hid = w_gate.shape[1]
    final = final_g is not None
    n_hblk = d // hcols
    assert n_hblk <= hid // th
    row = lambda b, i, j: (b, i, 0)
    in_specs = [pl.BlockSpec((None, tm, d), row),
                pl.BlockSpec((None, tm, hcols), lambda b, i, j: (b, i, jnp.minimum(j, n_hblk - 1))),
                pl.BlockSpec((None, 1, d), _mod_spec(row0, 5)),
                pl.BlockSpec((d, th), lambda b, i, j: (0, j)),
                pl.BlockSpec((d, th), lambda b, i, j: (0, j)),
                pl.BlockSpec((th, d), lambda b, i, j: (j, 0))]
    args = [hn, h, mods3, w_gate, w_up, w_down]
    if final:
        in_specs.append(_const_spec((1, d)))
        args.append(final_g.reshape(1, d))
    return pl.pallas_call(
        functools.partial(_ffn_kernel, final=final, rows=rows, ncols=ncols),
        grid=(bx, lx // tm, hid // th),
        in_specs=in_specs,
        out_specs=pl.BlockSpec((None, tm, d), row),
        out_shape=jax.ShapeDtypeStruct((bx, lx, d), F32),
        name="ffn",
        compiler_params=_params(("arbitrary", "arbitrary", "arbitrary")),
    )(*args)


def _rope_tables(n):
    rows = n // GRID_W
    row = jnp.broadcast_to(jnp.arange(rows, dtype=F32)[:, None], (rows, GRID_W)).reshape(-1)
    col = jnp.broadcast_to(jnp.arange(GRID_W, dtype=F32)[None, :], (rows, GRID_W)).reshape(-1)
    inv = jnp.power(ROPE_THETA, -jnp.arange(0, AXIS_DIM, 2, dtype=F32) / AXIS_DIM)
    ang_r = row[:, None] * inv
    ang_c = col[:, None] * inv
    cos = jnp.concatenate([jnp.cos(ang_r), jnp.cos(ang_c), jnp.cos(ang_r), jnp.cos(ang_c)], axis=-1)
    sin = jnp.concatenate([-jnp.sin(ang_r), -jnp.sin(ang_c), jnp.sin(ang_r), jnp.sin(ang_c)], axis=-1)
    return cos, sin


def _head_layout(t):
    lead = t.shape[:-1]
    n = t.shape[-1] // HEAD_DIM
    t = t.reshape(lead + (n, 2, 2, HEAD_DIM // 4))
    return jnp.swapaxes(t, -2, -3).reshape(lead + (n * HEAD_DIM,))


def _split_w_in(w_in):
    nqk = ATT_WIDTH + KV_WIDTH
    return _head_layout(w_in[:, :nqk]).astype(BF16), w_in[:, nqk:].astype(BF16)


def kernel(x, c, ctx, c_ctx, l0_norm1_g, l0_w_mod, l0_b_mod, l0_w_in, l0_q_norm_g, l0_k_norm_g, l0_conv_w, l0_w_out, l0_norm2_g, l0_w_gate, l0_w_up, l0_w_down, l1_norm1_g, l1_w_mod, l1_b_mod, l1_w_in, l1_q_norm_g, l1_k_norm_g, l1_sink, l1_pool_w, l1_pool_scale, l1_w_out, l1_norm2_g, l1_w_gate, l1_w_up, l1_w_down, final_norm_g):
    b, s, d = x.shape
    lc = ctx.shape[1]
    rope = _rope_tables(s)
    c_all = jnp.zeros((MOD_ROWS, d), F32).at[:b].set(c).at[b].set(c_ctx)
    ctx_row = b
    bf = lambda w: w.astype(BF16)
    tm_in, tm_out = 512, 512
    ffn_tiles = dict(tm=1024, th=512, rows=512, ncols=512, hcols=256)

    mods = _modulation(c_all, l0_w_mod, l0_b_mod).reshape(MOD_ROWS, 1, N_MOD * d)
    w_qk, w_rest = _split_w_in(l0_w_in)
    w_out = bf(l0_w_out)
    q_g, k_g = _head_layout(l0_q_norm_g), _head_layout(l0_k_norm_g)
    w_gate, w_up, w_down = bf(l0_w_gate), bf(l0_w_up), bf(l0_w_down)
    hc = ctx.reshape(1, b * lc, d)

    q, k, v, aux = _inproj(x, mods, 0, l0_norm1_g, w_qk, w_rest, q_g, k_g, rope, tm=tm_in)
    qc, kc, vc, auxc = _inproj(hc, mods, ctx_row, l0_norm1_g, w_qk, w_rest, q_g, k_g, None, tm=tm_in)
    qc, kc, vc = (t.reshape(b, lc, -1) for t in (qc, kc, vc))

    attn = _attention(q, k, v, kc, vc, None, mode="dense", tq=256, nsub=2, tk=512)
    conv = _short_conv(aux, l0_conv_w)
    h, hn = _outproj(attn, conv, w_out, x, mods, 0, l0_norm2_g, tm=tm_out)
    h = _ffn(hn, h, mods, 0, w_gate, w_up, w_down, None, **ffn_tiles)

    attn_c = _attention(qc, None, None, kc, vc, None, mode="ctx", tq=lc, nsub=1, tk=lc)
    conv_c = _short_conv(auxc.reshape(b, lc, -1), l0_conv_w)
    hc, hcn = _outproj(attn_c.reshape(1, b * lc, -1), conv_c.reshape(1, b * lc, -1), w_out, hc, mods, ctx_row,
                       l0_norm2_g, tm=tm_out)
    hc = _ffn(hcn, hc, mods, ctx_row, w_gate, w_up, w_down, None, **ffn_tiles)

    mods = _modulation(c_all, l1_w_mod, l1_b_mod).reshape(MOD_ROWS, 1, N_MOD * d)
    w_qk, w_rest = _split_w_in(l1_w_in)
    w_out = bf(l1_w_out)
    q_g, k_g = _head_layout(l1_q_norm_g), _head_layout(l1_k_norm_g)
    w_gate, w_up, w_down = bf(l1_w_gate), bf(l1_w_up), bf(l1_w_down)

    q, k, v, u = _inproj(h, mods, 0, l1_norm1_g, w_qk, w_rest, q_g, k_g, rope, tm=tm_in)
    kc, vc = _inproj(hc, mods, ctx_row, l1_norm1_g, w_qk, w_rest, q_g, k_g, None, tm=tm_in,
                     want_q=False, want_aux=False)
    kc, vc = kc.reshape(b, lc, -1), vc.reshape(b, lc, -1)

    attn = _attention(q, k, v, kc, vc, l1_sink, mode="window", tq=256, nsub=2, tk=512)
    pool = _multiscale_pool(u, bf(l1_pool_w), l1_pool_scale)
    h, hn = _outproj(attn, pool, w_out, h, mods, 0, l1_norm2_g, tm=tm_out)
    return _ffn(hn, h, mods, 0, w_gate, w_up, w_down, final_norm_g, **ffn_tiles)
```

```python
import functools
import math

import jax
import jax.numpy as jnp
import numpy as np
from jax import lax
from jax.experimental import pallas as pl
from jax.experimental.pallas import tpu as pltpu

F32 = jnp.float32
BF16 = jnp.bfloat16

LANES = 128
HEAD_DIM = 128
AXIS_DIM = HEAD_DIM // 2
Q_HEADS = 12
KV_HEADS = 4
GROUPS = Q_HEADS // KV_HEADS
ATT_WIDTH = Q_HEADS * HEAD_DIM
KV_WIDTH = KV_HEADS * HEAD_DIM
GRID_W = 64
WINDOW = 128
POOL_WINDOWS = (2, 4, 8, 16)
ROPE_THETA = 10000.0
N_MOD = 6
EPS = 1e-6
NEG_INF = -1e30
ATT_SCALE = HEAD_DIM ** -0.5
LOG2E = math.log2(math.e)
MOD_ROWS = 8
VMEM_LIMIT = 56 * 1024 * 1024
NORM_ROWS = 16
HALO = 8
SCORE_LOOKAHEAD = 2
VT_ROWS = HEAD_DIM + 16


def _params(sem):
    return pltpu.CompilerParams(dimension_semantics=sem, vmem_limit_bytes=VMEM_LIMIT)


def _mod_spec(row0, chunk):
    def imap(b, *_):
        return (b + row0, 0, chunk)
    return imap


def _const_spec(shape):
    zeros = (0,) * len(shape)
    return pl.BlockSpec(shape, lambda *_: zeros, pipeline_mode=pl.Buffered(1))


def _mod_kernel(c_ref, w_ref, b_ref, o_ref):
    c = c_ref[...]
    sc = c * jax.nn.sigmoid(c)
    o_ref[...] = jnp.dot(sc.astype(BF16), w_ref[...].astype(BF16),
                         preferred_element_type=F32) + b_ref[...]


def _modulation(c_all, w_mod, b_mod):
    d, n = w_mod.shape
    tn = 1024
    return pl.pallas_call(
        _mod_kernel,
        grid=(n // tn,),
        in_specs=[pl.BlockSpec((MOD_ROWS, d), lambda j: (0, 0)),
                  pl.BlockSpec((d, tn), lambda j: (0, j)),
                  pl.BlockSpec((1, tn), lambda j: (0, j))],
        out_specs=pl.BlockSpec((MOD_ROWS, tn), lambda j: (0, j)),
        out_shape=jax.ShapeDtypeStruct((MOD_ROWS, n), F32),
        name="modulation",
        compiler_params=_params(("arbitrary",)),
    )(c_all, w_mod, b_mod.reshape(1, n))


def _rms_modulate_rows(x, gs, shift):
    ms = jnp.mean(x * x, axis=-1, keepdims=True)
    return x * lax.rsqrt(ms + EPS) * gs + shift


def _norm_to(dst_ref, src, gs, shift, row0=0):
    rows = src.shape[0]
    for r in range(0, rows, NORM_ROWS):
        x = src[r:r + NORM_ROWS, :]
        dst_ref[row0 + r:row0 + r + NORM_ROWS, :] = _rms_modulate_rows(x, gs, shift).astype(BF16)


def _head_norm_rope(y, g, cos, sin):
    outs = []
    for hh in range(y.shape[1] // HEAD_DIM):
        t = y[:, hh * HEAD_DIM:(hh + 1) * HEAD_DIM]
        ms = jnp.mean(t * t, axis=-1, keepdims=True)
        t = t * lax.rsqrt(ms + EPS) * g
        if cos is not None:
            t = t * cos + pltpu.roll(t, AXIS_DIM, 1) * sin
        outs.append(t)
    return jnp.concatenate(outs, axis=1)


def _inproj_kernel(*refs, tn, nq, rope, want_q, want_aux):
    it = iter(refs)
    h_ref, shift_ref, scale_ref, g1_ref, wqk_ref, wr_ref, qg_ref, kg_ref = (next(it) for _ in range(8))
    cos_ref = sin_ref = None
    if rope:
        cos_ref, sin_ref = next(it), next(it)
    q_ref = next(it) if want_q else None
    k_ref, v_ref = next(it), next(it)
    aux_ref = next(it) if want_aux else None
    xn_ref = next(it)

    _norm_to(xn_ref, h_ref, g1_ref[...] * (1.0 + scale_ref[...]), shift_ref[...])
    cos = cos_ref[...] if rope else None
    sin = sin_ref[...] if rope else None

    def tile(w_ref, j):
        return jnp.dot(xn_ref[...], w_ref[:, j * tn:(j + 1) * tn], preferred_element_type=F32)

    if want_q:
        qg = qg_ref[...] * (ATT_SCALE * LOG2E)
        for j in range(nq):
            q_ref[:, j * tn:(j + 1) * tn] = _head_norm_rope(tile(wqk_ref, j), qg, cos, sin).astype(BF16)
    k_ref[...] = _head_norm_rope(tile(wqk_ref, nq), kg_ref[...], cos, sin).astype(BF16)
    v_ref[...] = tile(wr_ref, 0).astype(BF16)
    if want_aux:
        for j in range(aux_ref.shape[1] // tn):
            aux_ref[:, j * tn:(j + 1) * tn] = tile(wr_ref, 1 + j)


def _inproj(h, mods3, row0, g1, w_qk, w_rest, q_g, k_g, rope_tabs, *, tm, want_q=True, want_aux=True):
    bx, lx, d = h.shape
    tn = KV_WIDTH
    nq = ATT_WIDTH // tn
    rope = rope_tabs is not None
    aux_w = w_rest.shape[1] - KV_WIDTH
    row = lambda b, i: (b, i, 0)

    in_specs = [
        pl.BlockSpec((None, tm, d), row),
        pl.BlockSpec((None, 1, d), _mod_spec(row0, 0)),
        pl.BlockSpec((None, 1, d), _mod_spec(row0, 1)),
        _const_spec((1, d)),
        _const_spec(w_qk.shape),
        _const_spec(w_rest.shape),
        _const_spec((1, HEAD_DIM)),
        _const_spec((1, HEAD_DIM)),
    ]
    args = [h, mods3, mods3, g1.reshape(1, d), w_qk, w_rest, q_g.reshape(1, HEAD_DIM), k_g.reshape(1, HEAD_DIM)]
    if rope:
        in_specs += [pl.BlockSpec((tm, HEAD_DIM), lambda b, i: (i, 0))] * 2
        args += list(rope_tabs)

    out_specs, out_shape = [], []
    if want_q:
        out_specs.append(pl.BlockSpec((None, tm, ATT_WIDTH), row))
        out_shape.append(jax.ShapeDtypeStruct((bx, lx, ATT_WIDTH), BF16))
    for _ in range(2):
        out_specs.append(pl.BlockSpec((None, tm, KV_WIDTH), row))
        out_shape.append(jax.ShapeDtypeStruct((bx, lx, KV_WIDTH), BF16))
    if want_aux:
        out_specs.append(pl.BlockSpec((None, tm, aux_w), row))
        out_shape.append(jax.ShapeDtypeStruct((bx, lx, aux_w), F32))

    return pl.pallas_call(
        functools.partial(_inproj_kernel, tn=tn, nq=nq, rope=rope, want_q=want_q, want_aux=want_aux),
        grid=(bx, lx // tm),
        in_specs=in_specs,
        out_specs=out_specs,
        out_shape=out_shape,
        scratch_shapes=[pltpu.VMEM((tm, d), BF16)],
        name="inproj",
        compiler_params=_params(("arbitrary", "arbitrary")),
    )(*args)


def _transpose_bf16(x):
    eye = (lax.broadcasted_iota(jnp.int32, (HEAD_DIM, HEAD_DIM), 0)
           == lax.broadcasted_iota(jnp.int32, (HEAD_DIM, HEAD_DIM), 1)).astype(BF16)
    return lax.dot_general(eye, x, (((1,), (1,)), ((), ())), preferred_element_type=F32).astype(BF16)


def _scores(kch, q, mask):
    s = lax.dot_general(kch, q, (((1,), (1,)), ((), ())), preferred_element_type=F32)
    return s if mask is None else jnp.where(mask, s, NEG_INF)


def _softmax_step(state, s, vt):
    m_prev, acc = state
    m_next = jnp.maximum(m_prev, jnp.max(s, axis=0, keepdims=True))
    p = jnp.exp2(s - m_next).astype(BF16)
    alpha = jnp.exp2(m_prev - m_next)
    acc = alpha * acc + jnp.dot(vt, p, preferred_element_type=F32)
    return m_next, acc


def _attn_kernel(*refs, mode, tq, nsub, tk, seq, has_sink):
    it = iter(refs)
    q_ref = next(it)
    k_ref = v_ref = None
    if mode != "ctx":
        k_ref, v_ref = next(it), next(it)
    kc_ref, vc_ref = next(it), next(it)
    sink_ref = next(it) if has_sink else None
    o_ref = next(it)
    vt_ref = next(it) if mode != "ctx" else None
    vct_ref = next(it)

    kvh = pl.program_id(1)
    step = pl.program_id(2)

    @pl.when(step == 0)
    def _():
        vct_ref[:HEAD_DIM, :] = _transpose_bf16(vc_ref[...])
        vct_ref[HEAD_DIM:, :] = jnp.ones((VT_ROWS - HEAD_DIM, vct_ref.shape[1]), BF16)
        if mode != "ctx":
            vt_ref[HEAD_DIM:, :] = jnp.ones((VT_ROWS - HEAD_DIM, seq), BF16)
            for c in range(0, seq, tk):
                vt_ref[:HEAD_DIM, c:c + tk] = _transpose_bf16(v_ref[c:c + tk, :])

    n = GROUPS * tq
    if mode == "window":
        band0 = (lax.broadcasted_iota(jnp.int32, (tk, n), 0)
                 - (lax.broadcasted_iota(jnp.int32, (tk, n), 1) & (tq - 1)) + WINDOW)
        interior_mask = (band0 - WINDOW).astype(jnp.uint32) <= 2 * WINDOW

    items = []
    for sub in range(nsub):
        start = (step * nsub + sub) * tq
        rows = slice(sub * tq, (sub + 1) * tq)
        q = jnp.concatenate([q_ref[rows, g * HEAD_DIM:(g + 1) * HEAD_DIM] for g in range(GROUPS)], axis=0)
        if mode == "dense":
            for c in range(0, seq, tk):
                items.append((sub, q, None, (lambda c=c: k_ref[c:c + tk, :]), (lambda c=c: vt_ref[:, c:c + tk])))
        elif mode == "window":
            ks = pl.multiple_of(jnp.clip(start - WINDOW, 0, seq - tk), WINDOW)
            if 0 < sub < nsub - 1:
                mask = interior_mask
            else:
                mask = (band0 + (ks - start)).astype(jnp.uint32) <= 2 * WINDOW
            items.append((sub, q, mask, (lambda ks=ks: k_ref[pl.ds(ks, tk), :]),
                          (lambda ks=ks: vt_ref[:, pl.ds(ks, tk)])))
        items.append((sub, q, None, (lambda: kc_ref[...]), (lambda: vct_ref[...])))

    per_sub = len(items) // nsub
    items = [items[sub * per_sub + c] for c in range(per_sub) for sub in range(nsub)]

    states = [(jnp.full((1, n), NEG_INF, F32), jnp.zeros((VT_ROWS, n), F32)) for _ in range(nsub)]
    score = lambda item: _scores(item[3](), item[1], item[2])
    pending = [score(item) for item in items[:SCORE_LOOKAHEAD]]
    for idx, (sub, _, _, _, vt_fn) in enumerate(items):
        s_cur = pending.pop(0)
        if idx + SCORE_LOOKAHEAD < len(items):
            pending.append(score(items[idx + SCORE_LOOKAHEAD]))
        states[sub] = _softmax_step(states[sub], s_cur, vt_fn())

    for sub, (m, acc) in enumerate(states):
        l = acc[HEAD_DIM:HEAD_DIM + 1, :]
        acc = acc[:HEAD_DIM, :]
        if has_sink:
            sink = jnp.concatenate([jnp.full((1, tq), sink_ref[kvh * GROUPS + g] * LOG2E, F32)
                                    for g in range(GROUPS)], axis=1)
            l = l + jnp.exp2(sink - m)
        out = acc / l
        for g in range(GROUPS):
            o_ref[sub * tq:(sub + 1) * tq, g * HEAD_DIM:(g + 1) * HEAD_DIM] = (
                out[:, g * tq:(g + 1) * tq].T.astype(BF16))


def _attention(q, k, v, kc, vc, sink, *, mode, tq, nsub, tk):
    b, lq, _ = q.shape
    lc = kc.shape[1]
    seq = k.shape[1] if k is not None else 0
    gw = GROUPS * HEAD_DIM
    tstep = tq * nsub
    in_specs = [pl.BlockSpec((None, tstep, gw), lambda bb, h, i: (bb, i, h))]
    args = [q]
    scratch = []
    if mode != "ctx":
        in_specs += [pl.BlockSpec((None, seq, HEAD_DIM), lambda bb, h, i: (bb, 0, h))] * 2
        args += [k, v]
        scratch.append(pltpu.VMEM((VT_ROWS, seq), BF16))
    in_specs += [pl.BlockSpec((None, lc, HEAD_DIM), lambda bb, h, i: (bb, 0, h))] * 2
    args += [kc, vc]
    scratch.append(pltpu.VMEM((VT_ROWS, lc), BF16))
    has_sink = sink is not None
    if has_sink:
        in_specs.append(pl.BlockSpec(memory_space=pltpu.SMEM))
        args.append(sink)
    return pl.pallas_call(
        functools.partial(_attn_kernel, mode=mode, tq=tq, nsub=nsub, tk=tk, seq=seq, has_sink=has_sink),
        grid=(b, KV_HEADS, lq // tstep),
        in_specs=in_specs,
        out_specs=pl.BlockSpec((None, tstep, gw), lambda bb, h, i: (bb, i, h)),
        out_shape=jax.ShapeDtypeStruct((b, lq, ATT_WIDTH), BF16),
        scratch_shapes=scratch,
        name="attn_" + mode,
        compiler_params=_params(("arbitrary", "arbitrary", "arbitrary")),
    )(*args)


def _conv_mixer(aux_ref, prev_ref, next_ref, w_ref, row0, seq_len):
    tm, w = aux_ref.shape[0], aux_ref.shape[1] // 3
    z = aux_ref[:, w:2 * w] * aux_ref[:, 2 * w:]
    z_prev = prev_ref[HALO - 1:HALO, w:2 * w] * prev_ref[HALO - 1:HALO, 2 * w:]
    z_next = next_ref[0:1, w:2 * w] * next_ref[0:1, 2 * w:]
    row = lax.broadcasted_iota(jnp.int32, (tm, w), 0)
    pos = (row0 + row) & (seq_len - 1)
    down = jnp.where(row == 0, z_prev, pltpu.roll(z, 1, 0))
    down = jnp.where(pos == 0, 0.0, down)
    up = jnp.where(row == tm - 1, z_next, pltpu.roll(z, tm - 1, 0))
    up = jnp.where(pos == seq_len - 1, 0.0, up)
    cw = w_ref[...]
    return aux_ref[:, :w] * (cw[0:1, :] * down + cw[1:2, :] * z + cw[2:3, :] * up)


def _pool_mixer(u_ref, prev_ref, next_ref, w_ref, s_ref, row0, seq_len):
    tm = u_ref.shape[0]
    at_start = (row0 & (seq_len - 1)) == 0
    at_end = ((row0 + tm) & (seq_len - 1)) == 0
    pos = (row0 & (seq_len - 1)) + lax.broadcasted_iota(jnp.int32, (tm, LANES), 0)
    outs = []
    for g, win in enumerate(POOL_WINDOWS):
        cols = slice(g * LANES, (g + 1) * LANES)
        u = u_ref[:, cols]
        ext = jnp.concatenate([jnp.where(at_start, 0.0, prev_ref[:, cols]), u,
                               jnp.where(at_end, 0.0, next_ref[:, cols])], axis=0)
        n = ext.shape[0]
        half = win // 2
        fwd = ext
        bwd = pltpu.roll(ext, 1, 0)
        k = 1
        while k < half:
            fwd = fwd + pltpu.roll(fwd, n - k, 0)
            bwd = bwd + pltpu.roll(bwd, k, 0)
            k *= 2
        cnt = jnp.minimum(pos + half, seq_len) - jnp.maximum(pos - half, 0)
        pooled = (fwd + bwd)[HALO:HALO + tm, :] / cnt.astype(F32) - u
        mixed = jnp.dot(pooled.astype(BF16), w_ref[g], preferred_element_type=F32)
        outs.append(mixed * s_ref[:, cols])
    return jnp.concatenate(outs, axis=1)


def _outproj_kernel(*refs, mixer, rows, seq_len):
    it = iter(refs)
    a_ref, aux_ref, prev_ref, next_ref = (next(it) for _ in range(4))
    mix_refs = [next(it) for _ in range(1 if mixer == "conv" else 2)]
    wa_ref, wx_ref, h_ref, gate_ref, shift_ref, scale_ref, g2_ref, o_ref, hn_ref, x_ref = it

    row0 = pl.program_id(1) * a_ref.shape[0]
    mix = _conv_mixer if mixer == "conv" else _pool_mixer
    gate = gate_ref[...]
    gs = g2_ref[...] * (1.0 + scale_ref[...])
    shift = shift_ref[...]
    for r in range(0, a_ref.shape[0], rows):
        rs = slice(r, r + rows)
        o_ref[rs, :] = h_ref[rs, :] + gate * jnp.dot(a_ref[rs, :], wa_ref[...], preferred_element_type=F32)
        if r == 0:
            x_ref[...] = mix(aux_ref, prev_ref, next_ref, *mix_refs, row0, seq_len).astype(BF16)
        o_ref[rs, :] += gate * jnp.dot(x_ref[rs, :], wx_ref[...], preferred_element_type=F32)
        _norm_to(hn_ref, o_ref.at[rs, :], gs, shift, row0=r)


def _outproj(attn, aux, mixer, mix_params, seq_len, w_out, h, mods3, row0, g2, *, tm):
    bx, lx, d = h.shape
    wa, aux_w = attn.shape[2], aux.shape[2]
    wx = d - wa
    row = lambda b, i: (b, i, 0)
    halo_blocks = tm // HALO
    mix_specs = [_const_spec(p.shape) for p in mix_params]
    return pl.pallas_call(
        functools.partial(_outproj_kernel, mixer=mixer, rows=tm // 2, seq_len=seq_len),
        grid=(bx, lx // tm),
        in_specs=[pl.BlockSpec((None, tm, wa), row),
                  pl.BlockSpec((None, tm, aux_w), row),
                  pl.BlockSpec((None, HALO, aux_w), lambda b, i: (b, jnp.maximum(i * halo_blocks - 1, 0), 0)),
                  pl.BlockSpec((None, HALO, aux_w),
                               lambda b, i: (b, jnp.minimum((i + 1) * halo_blocks, lx // HALO - 1), 0)),
                  *mix_specs,
                  pl.BlockSpec((wa, d), lambda b, i: (0, 0), pipeline_mode=pl.Buffered(1)),
                  pl.BlockSpec((wx, d), lambda b, i: (wa // wx, 0), pipeline_mode=pl.Buffered(1)),
                  pl.BlockSpec((None, tm, d), row),
                  pl.BlockSpec((None, 1, d), _mod_spec(row0, 2)),
                  pl.BlockSpec((None, 1, d), _mod_spec(row0, 3)),
                  pl.BlockSpec((None, 1, d), _mod_spec(row0, 4)),
                  _const_spec((1, d))],
        out_specs=[pl.BlockSpec((None, tm, d), row), pl.BlockSpec((None, tm, d), row)],
        out_shape=[jax.ShapeDtypeStruct((bx, lx, d), F32), jax.ShapeDtypeStruct((bx, lx, d), BF16)],
        scratch_shapes=[pltpu.VMEM((tm, wx), BF16)],
        name="outproj_" + mixer,
        compiler_params=_params(("arbitrary", "arbitrary")),
    )(attn, aux, aux, aux, *mix_params, w_out, w_out, h, mods3, mods3, mods3, g2.reshape(1, d))


def _ffn_kernel(*refs, final, rows, ncols):
    it = iter(refs)
    hn_ref, h_ref, gate_ref, wg_ref, wu_ref, wd_ref = (next(it) for _ in range(6))
    fg_ref = next(it) if final else None
    o_ref = next(it)
    j = pl.program_id(2)
    tm, d = o_ref.shape
    hcols = h_ref.shape[1]
    n_hblk = d // hcols

    @pl.when(j == 0)
    def _():
        o_ref[...] = jnp.zeros(o_ref.shape, F32)

    col = pl.multiple_of(jnp.minimum(j, n_hblk - 1) * hcols, hcols)
    o_ref[:, pl.ds(col, hcols)] += jnp.where(j < n_hblk, h_ref[...], 0.0)

    gate = gate_ref[...]
    for r in range(0, tm, rows):
        rs = slice(r, r + rows)
        hn = hn_ref[rs, :]
        gt = jnp.dot(hn, wg_ref[...], preferred_element_type=F32)
        up = jnp.dot(hn, wu_ref[...], preferred_element_type=F32)
        act = (gt * jax.nn.sigmoid(gt) * up).astype(BF16)
        for c in range(0, d, ncols):
            cs = slice(c, c + ncols)
            o_ref[rs, cs] += gate[:, cs] * jnp.dot(act, wd_ref[:, cs], preferred_element_type=F32)

    if final:
        @pl.when(j == pl.num_programs(2) - 1)
        def _():
            fg = fg_ref[...]
            for r in range(0, o_ref.shape[0], NORM_ROWS):
                x = o_ref[r:r + NORM_ROWS, :]
                ms = jnp.mean(x * x, axis=-1, keepdims=True)
                o_ref[r:r + NORM_ROWS, :] = x * lax.rsqrt(ms + EPS) * fg


def _ffn(hn, h, mods3, row0, w_gate, w_up, w_down, final_g, *, tm, th, rows, ncols, hcols):
    bx, lx, d = h.shape
    hid = w_gate.shape[1]
    final = final_g is not None
    n_hblk = d // hcols
    assert n_hblk <= hid // th
    row = lambda b, i, j: (b, i, 0)
    in_specs = [pl.BlockSpec((None, tm, d), row),
                pl.BlockSpec((None, tm, hcols), lambda b, i, j: (b, i, jnp.minimum(j, n_hblk - 1))),
                pl.BlockSpec((None, 1, d), _mod_spec(row0, 5)),
                pl.BlockSpec((d, th), lambda b, i, j: (0, j)),
                pl.BlockSpec((d, th), lambda b, i, j: (0, j)),
                pl.BlockSpec((th, d), lambda b, i, j: (j, 0))]
    args = [hn, h, mods3, w_gate, w_up, w_down]
    if final:
        in_specs.append(_const_spec((1, d)))
        args.append(final_g.reshape(1, d))
    return pl.pallas_call(
        functools.partial(_ffn_kernel, final=final, rows=rows, ncols=ncols),
        grid=(bx, lx // tm, hid // th),
        in_specs=in_specs,
        out_specs=pl.BlockSpec((None, tm, d), row),
        out_shape=jax.ShapeDtypeStruct((bx, lx, d), F32),
        name="ffn",
        compiler_params=_params(("arbitrary", "arbitrary", "arbitrary")),
    )(*args)


def _rope_tables(n):
    pos = np.arange(n)
    inv = np.power(ROPE_THETA, -np.arange(0, AXIS_DIM, 2, dtype=np.float64) / AXIS_DIM)
    ang_r = (pos // GRID_W)[:, None] * inv
    ang_c = (pos % GRID_W)[:, None] * inv
    cos = np.concatenate([np.cos(ang_r), np.cos(ang_c), np.cos(ang_r), np.cos(ang_c)], axis=-1)
    sin = np.concatenate([-np.sin(ang_r), -np.sin(ang_c), np.sin(ang_r), np.sin(ang_c)], axis=-1)
    return jnp.asarray(cos, F32), jnp.asarray(sin, F32)


def _head_layout(t):
    lead = t.shape[:-1]
    n = t.shape[-1] // HEAD_DIM
    t = t.reshape(lead + (n, 2, 2, HEAD_DIM // 4))
    return jnp.swapaxes(t, -2, -3).reshape(lead + (n * HEAD_DIM,))


def _split_w_in(w_in):
    nqk = ATT_WIDTH + KV_WIDTH
    w = w_in.astype(BF16)
    return _head_layout(w[:, :nqk]), w[:, nqk:]


def kernel(x, c, ctx, c_ctx, l0_norm1_g, l0_w_mod, l0_b_mod, l0_w_in, l0_q_norm_g, l0_k_norm_g, l0_conv_w, l0_w_out, l0_norm2_g, l0_w_gate, l0_w_up, l0_w_down, l1_norm1_g, l1_w_mod, l1_b_mod, l1_w_in, l1_q_norm_g, l1_k_norm_g, l1_sink, l1_pool_w, l1_pool_scale, l1_w_out, l1_norm2_g, l1_w_gate, l1_w_up, l1_w_down, final_norm_g):
    b, s, d = x.shape
    lc = ctx.shape[1]
    rope = _rope_tables(s)
    c_all = jnp.zeros((MOD_ROWS, d), F32).at[:b].set(c).at[b].set(c_ctx)
    ctx_row = b
    bf = lambda w: w.astype(BF16)
    tm_in, tm_out = 512, 512
    ffn_tiles = dict(tm=1024, th=512, rows=512, ncols=512, hcols=256)

    mods = _modulation(c_all, l0_w_mod, l0_b_mod).reshape(MOD_ROWS, 1, N_MOD * d)
    w_qk, w_rest = _split_w_in(l0_w_in)
    w_out = bf(l0_w_out)
    q_g, k_g = _head_layout(l0_q_norm_g), _head_layout(l0_k_norm_g)
    w_gate, w_up, w_down = bf(l0_w_gate), bf(l0_w_up), bf(l0_w_down)
    hc = ctx.reshape(1, b * lc, d)

    q, k, v, aux = _inproj(x, mods, 0, l0_norm1_g, w_qk, w_rest, q_g, k_g, rope, tm=tm_in)
    qc, kc, vc, auxc = _inproj(hc, mods, ctx_row, l0_norm1_g, w_qk, w_rest, q_g, k_g, None, tm=tm_in)
    qc, kc, vc = (t.reshape(b, lc, -1) for t in (qc, kc, vc))

    attn = _attention(q, k, v, kc, vc, None, mode="dense", tq=256, nsub=2, tk=512)
    h, hn = _outproj(attn, aux, "conv", (l0_conv_w,), s, w_out, x, mods, 0, l0_norm2_g, tm=tm_out)
    h = _ffn(hn, h, mods, 0, w_gate, w_up, w_down, None, **ffn_tiles)

    attn_c = _attention(qc, None, None, kc, vc, None, mode="ctx", tq=lc, nsub=1, tk=lc)
    hc, hcn = _outproj(attn_c.reshape(1, b * lc, -1), auxc, "conv", (l0_conv_w,), lc, w_out, hc, mods, ctx_row,
                       l0_norm2_g, tm=tm_out)
    hc = _ffn(hcn, hc, mods, ctx_row, w_gate, w_up, w_down, None, **ffn_tiles)

    mods = _modulation(c_all, l1_w_mod, l1_b_mod).reshape(MOD_ROWS, 1, N_MOD * d)
    w_qk, w_rest = _split_w_in(l1_w_in)
    w_out = bf(l1_w_out)
    q_g, k_g = _head_layout(l1_q_norm_g), _head_layout(l1_k_norm_g)
    w_gate, w_up, w_down = bf(l1_w_gate), bf(l1_w_up), bf(l1_w_down)

    q, k, v, u = _inproj(h, mods, 0, l1_norm1_g, w_qk, w_rest, q_g, k_g, rope, tm=tm_in)
    kc, vc = _inproj(hc, mods, ctx_row, l1_norm1_g, w_qk, w_rest, q_g, k_g, None, tm=tm_in,
                     want_q=False, want_aux=False)
    kc, vc = kc.reshape(b, lc, -1), vc.reshape(b, lc, -1)

    attn = _attention(q, k, v, kc, vc, l1_sink, mode="window", tq=256, nsub=8, tk=512)
    pool_params = (bf(l1_pool_w), l1_pool_scale.reshape(1, -1))
    h, hn = _outproj(attn, u, "pool", pool_params, s, w_out, h, mods, 0, l1_norm2_g, tm=tm_out)
    return _ffn(hn, h, mods, 0, w_gate, w_up, w_down, final_norm_g, **ffn_tiles)
```

```python
import functools
import math

import jax
import jax.numpy as jnp
import numpy as np
from jax import lax
from jax.experimental import pallas as pl
from jax.experimental.pallas import tpu as pltpu

F32 = jnp.float32
BF16 = jnp.bfloat16

LANES = 128
HEAD_DIM = 128
AXIS_DIM = HEAD_DIM // 2
Q_HEADS = 12
KV_HEADS = 4
GROUPS = Q_HEADS // KV_HEADS
ATT_WIDTH = Q_HEADS * HEAD_DIM
KV_WIDTH = KV_HEADS * HEAD_DIM
GRID_W = 64
WINDOW = 128
POOL_WINDOWS = (2, 4, 8, 16)
ROPE_THETA = 10000.0
N_MOD = 6
EPS = 1e-6
NEG_INF = -1e30
ATT_SCALE = HEAD_DIM ** -0.5
LOG2E = math.log2(math.e)
MOD_ROWS = 8
VMEM_LIMIT = 56 * 1024 * 1024
NORM_ROWS = 16
HALO = 8
SCORE_LOOKAHEAD = 2
VT_ROWS = HEAD_DIM + 16


def _params(sem):
    return pltpu.CompilerParams(dimension_semantics=sem, vmem_limit_bytes=VMEM_LIMIT)


def _mod_spec(row0, chunk):
    def imap(b, *_):
        return (b + row0, 0, chunk)
    return imap


def _const_spec(shape):
    zeros = (0,) * len(shape)
    return pl.BlockSpec(shape, lambda *_: zeros, pipeline_mode=pl.Buffered(1))


def _mod_kernel(c_ref, w_ref, b_ref, o_ref):
    c = c_ref[...]
    sc = c * jax.nn.sigmoid(c)
    o_ref[...] = jnp.dot(sc.astype(BF16), w_ref[...].astype(BF16),
                         preferred_element_type=F32) + b_ref[...]


def _modulation(c_all, w_mod, b_mod):
    d, n = w_mod.shape
    tn = 1024
    return pl.pallas_call(
        _mod_kernel,
        grid=(n // tn,),
        in_specs=[pl.BlockSpec((MOD_ROWS, d), lambda j: (0, 0)),
                  pl.BlockSpec((d, tn), lambda j: (0, j)),
                  pl.BlockSpec((1, tn), lambda j: (0, j))],
        out_specs=pl.BlockSpec((MOD_ROWS, tn), lambda j: (0, j)),
        out_shape=jax.ShapeDtypeStruct((MOD_ROWS, n), F32),
        name="modulation",
        compiler_params=_params(("arbitrary",)),
    )(c_all, w_mod, b_mod.reshape(1, n))


def _rms_modulate_rows(x, gs, shift):
    ms = jnp.mean(x * x, axis=-1, keepdims=True)
    return x * lax.rsqrt(ms + EPS) * gs + shift


def _norm_to(dst_ref, src, gs, shift, row0=0):
    rows = src.shape[0]
    for r in range(0, rows, NORM_ROWS):
        x = src[r:r + NORM_ROWS, :]
        dst_ref[row0 + r:row0 + r + NORM_ROWS, :] = _rms_modulate_rows(x, gs, shift).astype(BF16)


def _head_norm_rope(y, g, cos, sin):
    lane = lax.broadcasted_iota(jnp.int32, (1, HEAD_DIM), 1)
    in_x1 = (lane & (AXIS_DIM // 2)) == 0
    outs = []
    for hh in range(y.shape[1] // HEAD_DIM):
        t = y[:, hh * HEAD_DIM:(hh + 1) * HEAD_DIM]
        ms = jnp.mean(t * t, axis=-1, keepdims=True)
        t = t * lax.rsqrt(ms + EPS) * g
        if cos is not None:
            partner = jnp.where(in_x1, pltpu.roll(t, HEAD_DIM - AXIS_DIM // 2, 1), pltpu.roll(t, AXIS_DIM // 2, 1))
            t = t * cos + partner * sin
        outs.append(t)
    return jnp.concatenate(outs, axis=1)


def _inproj_kernel(*refs, tn, nq, rope, want_q, want_aux):
    it = iter(refs)
    h_ref, shift_ref, scale_ref, g1_ref, wqk_ref, wr_ref, qg_ref, kg_ref = (next(it) for _ in range(8))
    cos_ref = sin_ref = None
    if rope:
        cos_ref, sin_ref = next(it), next(it)
    q_ref = next(it) if want_q else None
    k_ref, v_ref = next(it), next(it)
    aux_ref = next(it) if want_aux else None
    xn_ref = next(it)

    _norm_to(xn_ref, h_ref, g1_ref[...] * (1.0 + scale_ref[...]), shift_ref[...])
    cos = cos_ref[...] if rope else None
    sin = sin_ref[...] if rope else None

    def tile(w_ref, j):
        return jnp.dot(xn_ref[...], w_ref[:, j * tn:(j + 1) * tn], preferred_element_type=F32)

    def q_tile(j):
        qg = qg_ref[...] * (ATT_SCALE * LOG2E)
        q_ref[:, j * tn:(j + 1) * tn] = _head_norm_rope(tile(wqk_ref, j), qg, cos, sin).astype(BF16)

    def k_tile():
        k_ref[...] = _head_norm_rope(tile(wqk_ref, nq), kg_ref[...], cos, sin).astype(BF16)

    def v_tile():
        v_ref[...] = tile(wr_ref, 0).astype(BF16)

    def aux_tile(j):
        aux_ref[:, j * tn:(j + 1) * tn] = tile(wr_ref, 1 + j)

    if want_q:
        for j in range(nq):
            q_tile(j)
    k_tile()
    v_tile()
    if want_aux:
        for j in range(aux_ref.shape[1] // tn):
            aux_tile(j)


def _inproj(h, mods3, row0, g1, w_in, q_g, k_g, rope_tabs, *, tm, want_q=True, want_aux=True):
    bx, lx, d = h.shape
    tn = KV_WIDTH
    nq = ATT_WIDTH // tn
    rope = rope_tabs is not None
    nqk = ATT_WIDTH + KV_WIDTH
    n_rest = w_in.shape[1] - nqk
    assert nqk % n_rest == 0
    aux_w = n_rest - KV_WIDTH
    row = lambda b, i: (b, i, 0)

    in_specs = [
        pl.BlockSpec((None, tm, d), row),
        pl.BlockSpec((None, 1, d), _mod_spec(row0, 0)),
        pl.BlockSpec((None, 1, d), _mod_spec(row0, 1)),
        _const_spec((1, d)),
        pl.BlockSpec((d, nqk), lambda b, i: (0, 0), pipeline_mode=pl.Buffered(1)),
        pl.BlockSpec((d, n_rest), lambda b, i: (0, nqk // n_rest), pipeline_mode=pl.Buffered(1)),
        _const_spec((1, HEAD_DIM)),
        _const_spec((1, HEAD_DIM)),
    ]
    args = [h, mods3, mods3, g1.reshape(1, d), w_in, w_in, q_g.reshape(1, HEAD_DIM), k_g.reshape(1, HEAD_DIM)]
    if rope:
        in_specs += [pl.BlockSpec((tm, HEAD_DIM), lambda b, i: (i, 0))] * 2
        args += list(rope_tabs)

    out_specs, out_shape = [], []
    if want_q:
        out_specs.append(pl.BlockSpec((None, tm, ATT_WIDTH), row))
        out_shape.append(jax.ShapeDtypeStruct((bx, lx, ATT_WIDTH), BF16))
    for _ in range(2):
        out_specs.append(pl.BlockSpec((None, tm, KV_WIDTH), row))
        out_shape.append(jax.ShapeDtypeStruct((bx, lx, KV_WIDTH), BF16))
    if want_aux:
        out_specs.append(pl.BlockSpec((None, tm, aux_w), row))
        out_shape.append(jax.ShapeDtypeStruct((bx, lx, aux_w), F32))

    return pl.pallas_call(
        functools.partial(_inproj_kernel, tn=tn, nq=nq, rope=rope, want_q=want_q, want_aux=want_aux),
        grid=(bx, lx // tm),
        in_specs=in_specs,
        out_specs=out_specs,
        out_shape=out_shape,
        scratch_shapes=[pltpu.VMEM((tm, d), BF16)],
        name="inproj",
        compiler_params=_params(("arbitrary", "arbitrary")),
    )(*args)


def _transpose_bf16(x):
    eye = (lax.broadcasted_iota(jnp.int32, (HEAD_DIM, HEAD_DIM), 0)
           == lax.broadcasted_iota(jnp.int32, (HEAD_DIM, HEAD_DIM), 1)).astype(BF16)
    return lax.dot_general(eye, x, (((1,), (1,)), ((), ())), preferred_element_type=F32).astype(BF16)


def _scores(kch, q, mask):
    s = lax.dot_general(kch, q, (((1,), (1,)), ((), ())), preferred_element_type=F32)
    return s if mask is None else jnp.where(mask, s, NEG_INF)


def _softmax_step(state, s, vt):
    m_prev, acc = state
    m_next = jnp.maximum(m_prev, jnp.max(s, axis=0, keepdims=True))
    p = jnp.exp2(s - m_next).astype(BF16)
    alpha = jnp.exp2(m_prev - m_next)
    acc = alpha * acc + jnp.dot(vt, p, preferred_element_type=F32)
    return m_next, acc


def _attn_kernel(*refs, mode, tq, nsub, tk, seq, has_sink):
    it = iter(refs)
    q_ref = next(it)
    k_ref = v_ref = None
    if mode != "ctx":
        k_ref, v_ref = next(it), next(it)
    kc_ref, vc_ref = next(it), next(it)
    sink_ref = next(it) if has_sink else None
    o_ref = next(it)
    vt_ref = next(it) if mode != "ctx" else None
    vct_ref = next(it)

    kvh = pl.program_id(1)
    step = pl.program_id(2)

    @pl.when(step == 0)
    def _():
        vct_ref[:HEAD_DIM, :] = _transpose_bf16(vc_ref[...])
        vct_ref[HEAD_DIM:, :] = jnp.ones((VT_ROWS - HEAD_DIM, vct_ref.shape[1]), BF16)
        if mode != "ctx":
            vt_ref[HEAD_DIM:, :] = jnp.ones((VT_ROWS - HEAD_DIM, seq), BF16)
            for c in range(0, seq, tk):
                vt_ref[:HEAD_DIM, c:c + tk] = _transpose_bf16(v_ref[c:c + tk, :])

    n = GROUPS * tq
    if mode == "window":
        band0 = (lax.broadcasted_iota(jnp.int32, (tk, n), 0)
                 - (lax.broadcasted_iota(jnp.int32, (tk, n), 1) & (tq - 1)) + WINDOW)
        interior_mask = (band0 - WINDOW).astype(jnp.uint32) <= 2 * WINDOW

    items = []
    for sub in range(nsub):
        start = (step * nsub + sub) * tq
        rows = slice(sub * tq, (sub + 1) * tq)
        q = jnp.concatenate([q_ref[rows, g * HEAD_DIM:(g + 1) * HEAD_DIM] for g in range(GROUPS)], axis=0)
        if mode == "dense":
            for c in range(0, seq, tk):
                items.append((sub, q, None, (lambda c=c: k_ref[c:c + tk, :]), (lambda c=c: vt_ref[:, c:c + tk])))
        elif mode == "window":
            ks = pl.multiple_of(jnp.clip(start - WINDOW, 0, seq - tk), WINDOW)
            if 0 < sub < nsub - 1:
                mask = interior_mask
            else:
                mask = (band0 + (ks - start)).astype(jnp.uint32) <= 2 * WINDOW
            items.append((sub, q, mask, (lambda ks=ks: k_ref[pl.ds(ks, tk), :]),
                          (lambda ks=ks: vt_ref[:, pl.ds(ks, tk)])))
        items.append((sub, q, None, (lambda: kc_ref[...]), (lambda: vct_ref[...])))

    per_sub = len(items) // nsub
    items = [items[sub * per_sub + c] for c in range(per_sub) for sub in range(nsub)]

    states = [(jnp.full((1, n), NEG_INF, F32), jnp.zeros((VT_ROWS, n), F32)) for _ in range(nsub)]
    score = lambda item: _scores(item[3](), item[1], item[2])
    pending = [score(item) for item in items[:SCORE_LOOKAHEAD]]
    for idx, (sub, _, _, _, vt_fn) in enumerate(items):
        s_cur = pending.pop(0)
        if idx + SCORE_LOOKAHEAD < len(items):
            pending.append(score(items[idx + SCORE_LOOKAHEAD]))
        states[sub] = _softmax_step(states[sub], s_cur, vt_fn())

    for sub, (m, acc) in enumerate(states):
        l = acc[HEAD_DIM:HEAD_DIM + 1, :]
        acc = acc[:HEAD_DIM, :]
        if has_sink:
            sink = jnp.concatenate([jnp.full((1, tq), sink_ref[kvh * GROUPS + g] * LOG2E, F32)
                                    for g in range(GROUPS)], axis=1)
            l = l + jnp.exp2(sink - m)
        out = acc / l
        for g in range(GROUPS):
            o_ref[sub * tq:(sub + 1) * tq, g * HEAD_DIM:(g + 1) * HEAD_DIM] = (
                out[:, g * tq:(g + 1) * tq].T.astype(BF16))


def _attention(q, k, v, kc, vc, sink, *, mode, tq, nsub, tk):
    b, lq, _ = q.shape
    lc = kc.shape[1]
    seq = k.shape[1] if k is not None else 0
    gw = GROUPS * HEAD_DIM
    tstep = tq * nsub
    in_specs = [pl.BlockSpec((None, tstep, gw), lambda bb, h, i: (bb, i, h))]
    args = [q]
    scratch = []
    if mode != "ctx":
        in_specs += [pl.BlockSpec((None, seq, HEAD_DIM), lambda bb, h, i: (bb, 0, h))] * 2
        args += [k, v]
        scratch.append(pltpu.VMEM((VT_ROWS, seq), BF16))
    in_specs += [pl.BlockSpec((None, lc, HEAD_DIM), lambda bb, h, i: (bb, 0, h))] * 2
    args += [kc, vc]
    scratch.append(pltpu.VMEM((VT_ROWS, lc), BF16))
    has_sink = sink is not None
    if has_sink:
        in_specs.append(pl.BlockSpec(memory_space=pltpu.SMEM))
        args.append(sink)
    return pl.pallas_call(
        functools.partial(_attn_kernel, mode=mode, tq=tq, nsub=nsub, tk=tk, seq=seq, has_sink=has_sink),
        grid=(b, KV_HEADS, lq // tstep),
        in_specs=in_specs,
        out_specs=pl.BlockSpec((None, tstep, gw), lambda bb, h, i: (bb, i, h)),
        out_shape=jax.ShapeDtypeStruct((b, lq, ATT_WIDTH), BF16),
        scratch_shapes=scratch,
        name="attn_" + mode,
        compiler_params=_params(("arbitrary", "arbitrary", "arbitrary")),
    )(*args)


def _conv_mixer(aux_ref, prev_ref, next_ref, w_ref, row0, seq_len):
    tm, w = aux_ref.shape[0], aux_ref.shape[1] // 3
    z = aux_ref[:, w:2 * w] * aux_ref[:, 2 * w:]
    z_prev = prev_ref[HALO - 1:HALO, w:2 * w] * prev_ref[HALO - 1:HALO, 2 * w:]
    z_next = next_ref[0:1, w:2 * w] * next_ref[0:1, 2 * w:]
    row = lax.broadcasted_iota(jnp.int32, (tm, w), 0)
    pos = (row0 + row) & (seq_len - 1)
    down = jnp.where(row == 0, z_prev, pltpu.roll(z, 1, 0))
    down = jnp.where(pos == 0, 0.0, down)
    up = jnp.where(row == tm - 1, z_next, pltpu.roll(z, tm - 1, 0))
    up = jnp.where(pos == seq_len - 1, 0.0, up)
    cw = w_ref[...]
    return aux_ref[:, :w] * (cw[0:1, :] * down + cw[1:2, :] * z + cw[2:3, :] * up)


def _pool_mixer(u_ref, prev_ref, next_ref, w_ref, s_ref, row0, seq_len):
    tm = u_ref.shape[0]
    at_start = (row0 & (seq_len - 1)) == 0
    at_end = ((row0 + tm) & (seq_len - 1)) == 0
    pos = (row0 & (seq_len - 1)) + lax.broadcasted_iota(jnp.int32, (tm, LANES), 0)
    outs = []
    for g, win in enumerate(POOL_WINDOWS):
        cols = slice(g * LANES, (g + 1) * LANES)
        u = u_ref[:, cols]
        ext = jnp.concatenate([jnp.where(at_start, 0.0, prev_ref[:, cols]), u,
                               jnp.where(at_end, 0.0, next_ref[:, cols])], axis=0)
        n = ext.shape[0]
        half = win // 2
        fwd = ext
        bwd = pltpu.roll(ext, 1, 0)
        k = 1
        while k < half:
            fwd = fwd + pltpu.roll(fwd, n - k, 0)
            bwd = bwd + pltpu.roll(bwd, k, 0)
            k *= 2
        cnt = jnp.minimum(pos + half, seq_len) - jnp.maximum(pos - half, 0)
        pooled = (fwd + bwd)[HALO:HALO + tm, :] / cnt.astype(F32) - u
        mixed = jnp.dot(pooled.astype(BF16), w_ref[g], preferred_element_type=F32)
        outs.append(mixed * s_ref[:, cols])
    return jnp.concatenate(outs, axis=1)


def _outproj_kernel(*refs, mixer, rows, seq_len):
    it = iter(refs)
    a_ref, aux_ref, prev_ref, next_ref = (next(it) for _ in range(4))
    mix_refs = [next(it) for _ in range(1 if mixer == "conv" else 2)]
    wa_ref, wx_ref, h_ref, gate_ref, shift_ref, scale_ref, g2_ref, o_ref, hn_ref, x_ref = it

    row0 = pl.program_id(1) * a_ref.shape[0]
    mix = _conv_mixer if mixer == "conv" else _pool_mixer
    gate = gate_ref[...]
    gs = g2_ref[...] * (1.0 + scale_ref[...])
    shift = shift_ref[...]
    for r in range(0, a_ref.shape[0], rows):
        rs = slice(r, r + rows)
        o_ref[rs, :] = h_ref[rs, :] + gate * jnp.dot(a_ref[rs, :], wa_ref[...], preferred_element_type=F32)
        if r == 0:
            x_ref[...] = mix(aux_ref, prev_ref, next_ref, *mix_refs, row0, seq_len).astype(BF16)
        o_ref[rs, :] += gate * jnp.dot(x_ref[rs, :], wx_ref[...], preferred_element_type=F32)
        _norm_to(hn_ref, o_ref.at[rs, :], gs, shift, row0=r)


def _outproj(attn, aux, mixer, mix_params, seq_len, w_out, h, mods3, row0, g2, *, tm):
    bx, lx, d = h.shape
    wa, aux_w = attn.shape[2], aux.shape[2]
    wx = d - wa
    row = lambda b, i: (b, i, 0)
    halo_blocks = tm // HALO
    mix_specs = [_const_spec(p.shape) for p in mix_params]
    return pl.pallas_call(
        functools.partial(_outproj_kernel, mixer=mixer, rows=tm // 2, seq_len=seq_len),
        grid=(bx, lx // tm),
        in_specs=[pl.BlockSpec((None, tm, wa), row),
                  pl.BlockSpec((None, tm, aux_w), row),
                  pl.BlockSpec((None, HALO, aux_w), lambda b, i: (b, jnp.maximum(i * halo_blocks - 1, 0), 0)),
                  pl.BlockSpec((None, HALO, aux_w),
                               lambda b, i: (b, jnp.minimum((i + 1) * halo_blocks, lx // HALO - 1), 0)),
                  *mix_specs,
                  pl.BlockSpec((wa, d), lambda b, i: (0, 0), pipeline_mode=pl.Buffered(1)),
                  pl.BlockSpec((wx, d), lambda b, i: (wa // wx, 0), pipeline_mode=pl.Buffered(1)),
                  pl.BlockSpec((None, tm, d), row),
                  pl.BlockSpec((None, 1, d), _mod_spec(row0, 2)),
                  pl.BlockSpec((None, 1, d), _mod_spec(row0, 3)),
                  pl.BlockSpec((None, 1, d), _mod_spec(row0, 4)),
                  _const_spec((1, d))],
        out_specs=[pl.BlockSpec((None, tm, d), row), pl.BlockSpec((None, tm, d), row)],
        out_shape=[jax.ShapeDtypeStruct((bx, lx, d), F32), jax.ShapeDtypeStruct((bx, lx, d), BF16)],
        scratch_shapes=[pltpu.VMEM((tm, wx), BF16)],
        name="outproj_" + mixer,
        compiler_params=_params(("arbitrary", "arbitrary")),
    )(attn, aux, aux, aux, *mix_params, w_out, w_out, h, mods3, mods3, mods3, g2.reshape(1, d))


def _ffn_kernel(*refs, final, th, rows, ncols):
    it = iter(refs)
    hn_ref, h_hbm, gate_ref, wg_hbm, wu_hbm, wd_hbm = (next(it) for _ in range(6))
    fg_ref = next(it) if final else None
    o_ref, wg_buf, wu_buf, wd_buf, w_sem, h_sem = (next(it) for _ in range(6))

    b, i = pl.program_id(0), pl.program_id(1)
    tm, d = o_ref.shape
    n_hid = wg_hbm.shape[1] // th
    tile = b * pl.num_programs(1) + i
    n_steps = pl.num_programs(0) * pl.num_programs(1) * n_hid
    step0 = tile * n_hid

    def w_copies(j, slot):
        col = j * th if isinstance(j, int) else pl.multiple_of(j * th, th)
        return (pltpu.make_async_copy(wg_hbm.at[:, pl.ds(col, th)], wg_buf.at[slot], w_sem.at[0, slot]),
                pltpu.make_async_copy(wu_hbm.at[:, pl.ds(col, th)], wu_buf.at[slot], w_sem.at[1, slot]),
                pltpu.make_async_copy(wd_hbm.at[pl.ds(col, th), :], wd_buf.at[slot], w_sem.at[2, slot]))

    h_copy = pltpu.make_async_copy(h_hbm.at[pl.ds(pl.multiple_of(tile * tm, tm), tm), :], o_ref, h_sem)
    h_copy.start()

    @pl.when(tile == 0)
    def _():
        for cp in w_copies(0, 0):
            cp.start()

    gate = gate_ref[...]

    def hidden_step(j, first):
        step = step0 + j
        slot = lax.rem(step, 2)
        for cp in w_copies(j, slot):
            cp.wait()

        @pl.when(step + 1 < n_steps)
        def _():
            nxt = jnp.where(j + 1 < n_hid, j + 1, 0)
            for cp in w_copies(nxt, 1 - slot):
                cp.start()

        wg, wu, wd = wg_buf.at[slot], wu_buf.at[slot], wd_buf.at[slot]
        for r in range(0, tm, rows):
            rs = slice(r, r + rows)
            hn = hn_ref[rs, :]
            gt = jnp.dot(hn, wg[...], preferred_element_type=F32)
            up = jnp.dot(hn, wu[...], preferred_element_type=F32)
            act = (gt * jax.nn.sigmoid(gt) * up).astype(BF16)
            if first and r == 0:
                h_copy.wait()
            for c in range(0, d, ncols):
                cs = slice(c, c + ncols)
                o_ref[rs, cs] += gate[:, cs] * jnp.dot(act, wd[:, cs], preferred_element_type=F32)

    hidden_step(0, True)

    def loop_body(j, carry):
        hidden_step(j, False)
        return carry

    lax.fori_loop(1, n_hid, loop_body, 0)

    if final:
        fg = fg_ref[...]
        for r in range(0, tm, NORM_ROWS):
            x = o_ref[r:r + NORM_ROWS, :]
            ms = jnp.mean(x * x, axis=-1, keepdims=True)
            o_ref[r:r + NORM_ROWS, :] = x * lax.rsqrt(ms + EPS) * fg


def _ffn(hn, h, mods3, row0, w_gate, w_up, w_down, final_g, *, tm, th, rows, ncols):
    bx, lx, d = h.shape
    final = final_g is not None
    row = lambda b, i: (b, i, 0)
    any_spec = pl.BlockSpec(memory_space=pl.ANY)
    in_specs = [pl.BlockSpec((None, tm, d), row), any_spec,
                pl.BlockSpec((None, 1, d), _mod_spec(row0, 5)),
                any_spec, any_spec, any_spec]
    args = [hn, h.reshape(bx * lx, d), mods3, w_gate, w_up, w_down]
    if final:
        in_specs.append(_const_spec((1, d)))
        args.append(final_g.reshape(1, d))
    return pl.pallas_call(
        functools.partial(_ffn_kernel, final=final, th=th, rows=rows, ncols=ncols),
        grid=(bx, lx // tm),
        in_specs=in_specs,
        out_specs=pl.BlockSpec((tm, d), lambda b, i: (b * (lx // tm) + i, 0)),
        out_shape=jax.ShapeDtypeStruct((bx * lx, d), F32),
        scratch_shapes=[pltpu.VMEM((2, d, th), BF16), pltpu.VMEM((2, d, th), BF16), pltpu.VMEM((2, th, d), BF16),
                        pltpu.SemaphoreType.DMA((3, 2)), pltpu.SemaphoreType.DMA(())],
        name="ffn",
        compiler_params=_params(("arbitrary", "arbitrary")),
    )(*args).reshape(bx, lx, d)


def _rope_tables(n):
    pos = np.arange(n)
    inv = np.power(ROPE_THETA, -np.arange(0, AXIS_DIM, 2, dtype=np.float64) / AXIS_DIM)
    ang_r = (pos // GRID_W)[:, None] * inv
    ang_c = (pos % GRID_W)[:, None] * inv
    cos = np.concatenate([np.cos(ang_r), np.cos(ang_r), np.cos(ang_c), np.cos(ang_c)], axis=-1)
    sin = np.concatenate([-np.sin(ang_r), np.sin(ang_r), -np.sin(ang_c), np.sin(ang_c)], axis=-1)
    return jnp.asarray(cos, F32), jnp.asarray(sin, F32)


def kernel(x, c, ctx, c_ctx, l0_norm1_g, l0_w_mod, l0_b_mod, l0_w_in, l0_q_norm_g, l0_k_norm_g, l0_conv_w, l0_w_out, l0_norm2_g, l0_w_gate, l0_w_up, l0_w_down, l1_norm1_g, l1_w_mod, l1_b_mod, l1_w_in, l1_q_norm_g, l1_k_norm_g, l1_sink, l1_pool_w, l1_pool_scale, l1_w_out, l1_norm2_g, l1_w_gate, l1_w_up, l1_w_down, final_norm_g):
    b, s, d = x.shape
    lc = ctx.shape[1]
    rope = _rope_tables(s)
    c_all = jnp.zeros((MOD_ROWS, d), F32).at[:b].set(c).at[b].set(c_ctx)
    ctx_row = b
    bf = lambda w: w.astype(BF16)
    tm_in, tm_out = 512, 512
    ffn_tiles = dict(tm=1024, th=512, rows=512, ncols=512)

    mods = _modulation(c_all, l0_w_mod, l0_b_mod).reshape(MOD_ROWS, 1, N_MOD * d)
    w_in, w_out = bf(l0_w_in), bf(l0_w_out)
    q_g, k_g = l0_q_norm_g, l0_k_norm_g
    w_gate, w_up, w_down = bf(l0_w_gate), bf(l0_w_up), bf(l0_w_down)
    hc = ctx.reshape(1, b * lc, d)

    q, k, v, aux = _inproj(x, mods, 0, l0_norm1_g, w_in, q_g, k_g, rope, tm=tm_in)
    qc, kc, vc, auxc = _inproj(hc, mods, ctx_row, l0_norm1_g, w_in, q_g, k_g, None, tm=tm_in)
    qc, kc, vc = (t.reshape(b, lc, -1) for t in (qc, kc, vc))

    attn = _attention(q, k, v, kc, vc, None, mode="dense", tq=256, nsub=2, tk=512)
    h, hn = _outproj(attn, aux, "conv", (l0_conv_w,), s, w_out, x, mods, 0, l0_norm2_g, tm=tm_out)
    h = _ffn(hn, h, mods, 0, w_gate, w_up, w_down, None, **ffn_tiles)

    attn_c = _attention(qc, None, None, kc, vc, None, mode="ctx", tq=lc, nsub=1, tk=lc)
    hc, hcn = _outproj(attn_c.reshape(1, b * lc, -1), auxc, "conv", (l0_conv_w,), lc, w_out, hc, mods, ctx_row,
                       l0_norm2_g, tm=tm_out)
    hc = _ffn(hcn, hc, mods, ctx_row, w_gate, w_up, w_down, None, **ffn_tiles)

    mods = _modulation(c_all, l1_w_mod, l1_b_mod).reshape(MOD_ROWS, 1, N_MOD * d)
    w_in, w_out = bf(l1_w_in), bf(l1_w_out)
    q_g, k_g = l1_q_norm_g, l1_k_norm_g
    w_gate, w_up, w_down = bf(l1_w_gate), bf(l1_w_up), bf(l1_w_down)

    q, k, v, u = _inproj(h, mods, 0, l1_norm1_g, w_in, q_g, k_g, rope, tm=tm_in)
    kc, vc = _inproj(hc, mods, ctx_row, l1_norm1_g, w_in, q_g, k_g, None, tm=tm_in,
                     want_q=False, want_aux=False)
    kc, vc = kc.reshape(b, lc, -1), vc.reshape(b, lc, -1)

    attn = _attention(q, k, v, kc, vc, l1_sink, mode="window", tq=256, nsub=8, tk=512)
    pool_params = (bf(l1_pool_w), l1_pool_scale.reshape(1, -1))
    h, hn = _outproj(attn, u, "pool", pool_params, s, w_out, h, mods, 0, l1_norm2_g, tm=tm_out)
    return _ffn(hn, h, mods, 0, w_gate, w_up, w_down, final_norm_g, **ffn_tiles)
```

```python
import functools
import math

import jax
import jax.numpy as jnp
import numpy as np
from jax import lax
from jax.experimental import pallas as pl
from jax.experimental.pallas import tpu as pltpu

F32 = jnp.float32
BF16 = jnp.bfloat16

LANES = 128
HEAD_DIM = 128
AXIS_DIM = HEAD_DIM // 2
Q_HEADS = 12
KV_HEADS = 4
GROUPS = Q_HEADS // KV_HEADS
ATT_WIDTH = Q_HEADS * HEAD_DIM
KV_WIDTH = KV_HEADS * HEAD_DIM
GRID_W = 64
WINDOW = 128
POOL_WINDOWS = (2, 4, 8, 16)
ROPE_THETA = 10000.0
N_MOD = 6
EPS = 1e-6
NEG_INF = -1e30
ATT_SCALE = HEAD_DIM ** -0.5
LOG2E = math.log2(math.e)
MOD_ROWS = 8
VMEM_LIMIT = 56 * 1024 * 1024
NORM_ROWS = 16
HALO = 8
SCORE_LOOKAHEAD = 2
VT_ROWS = HEAD_DIM + 16


def _params(sem):
    return pltpu.CompilerParams(dimension_semantics=sem, vmem_limit_bytes=VMEM_LIMIT)


def _mod_spec(row0, chunk):
    def imap(b, *_):
        return (b + row0, 0, chunk)
    return imap


def _const_spec(shape):
    zeros = (0,) * len(shape)
    return pl.BlockSpec(shape, lambda *_: zeros, pipeline_mode=pl.Buffered(1))


def _mod_kernel(c_ref, w_ref, b_ref, o_ref):
    c = c_ref[...]
    sc = c * jax.nn.sigmoid(c)
    o_ref[...] = jnp.dot(sc.astype(BF16), w_ref[...].astype(BF16),
                         preferred_element_type=F32) + b_ref[...]


def _modulation(c_all, w_mod, b_mod):
    d, n = w_mod.shape
    tn = 1024
    return pl.pallas_call(
        _mod_kernel,
        grid=(n // tn,),
        in_specs=[pl.BlockSpec((MOD_ROWS, d), lambda j: (0, 0)),
                  pl.BlockSpec((d, tn), lambda j: (0, j)),
                  pl.BlockSpec((1, tn), lambda j: (0, j))],
        out_specs=pl.BlockSpec((MOD_ROWS, tn), lambda j: (0, j)),
        out_shape=jax.ShapeDtypeStruct((MOD_ROWS, n), F32),
        name="modulation",
        compiler_params=_params(("arbitrary",)),
    )(c_all, w_mod, b_mod.reshape(1, n))


def _rms_modulate_rows(x, gs, shift):
    ms = jnp.mean(x * x, axis=-1, keepdims=True)
    return x * lax.rsqrt(ms + EPS) * gs + shift


def _norm_to(dst_ref, src, gs, shift, row0=0):
    rows = src.shape[0]
    for r in range(0, rows, NORM_ROWS):
        x = src[r:r + NORM_ROWS, :]
        dst_ref[row0 + r:row0 + r + NORM_ROWS, :] = _rms_modulate_rows(x, gs, shift).astype(BF16)


def _head_norm_rope(y, g, cos, sin):
    lane = lax.broadcasted_iota(jnp.int32, (1, HEAD_DIM), 1)
    in_x1 = (lane & (AXIS_DIM // 2)) == 0
    outs = []
    for hh in range(y.shape[1] // HEAD_DIM):
        t = y[:, hh * HEAD_DIM:(hh + 1) * HEAD_DIM]
        ms = jnp.mean(t * t, axis=-1, keepdims=True)
        t = t * lax.rsqrt(ms + EPS) * g
        if cos is not None:
            partner = jnp.where(in_x1, pltpu.roll(t, HEAD_DIM - AXIS_DIM // 2, 1), pltpu.roll(t, AXIS_DIM // 2, 1))
            t = t * cos + partner * sin
        outs.append(t)
    return jnp.concatenate(outs, axis=1)


def _inproj_kernel(*refs, tn, nq, rope, want_q, want_aux):
    it = iter(refs)
    h_ref, shift_ref, scale_ref, g1_ref, wqk_ref, wr_ref, qg_ref, kg_ref = (next(it) for _ in range(8))
    cos_ref = sin_ref = None
    if rope:
        cos_ref, sin_ref = next(it), next(it)
    q_ref = next(it) if want_q else None
    k_ref, v_ref = next(it), next(it)
    aux_ref = next(it) if want_aux else None
    xn_ref = next(it)

    _norm_to(xn_ref, h_ref, g1_ref[...] * (1.0 + scale_ref[...]), shift_ref[...])
    cos = cos_ref[...] if rope else None
    sin = sin_ref[...] if rope else None

    def tile(w_ref, j):
        return jnp.dot(xn_ref[...], w_ref[:, j * tn:(j + 1) * tn], preferred_element_type=F32)

    def q_tile(j):
        qg = qg_ref[...] * (ATT_SCALE * LOG2E)
        q_ref[:, j * tn:(j + 1) * tn] = _head_norm_rope(tile(wqk_ref, j), qg, cos, sin).astype(BF16)

    def k_tile():
        k_ref[...] = _head_norm_rope(tile(wqk_ref, nq), kg_ref[...], cos, sin).astype(BF16)

    def v_tile():
        v_ref[...] = tile(wr_ref, 0).astype(BF16)

    def aux_tile(j):
        aux_ref[:, j * tn:(j + 1) * tn] = tile(wr_ref, 1 + j)

    if want_q:
        for j in range(nq):
            q_tile(j)
    k_tile()
    v_tile()
    if want_aux:
        for j in range(aux_ref.shape[1] // tn):
            aux_tile(j)


def _inproj(h, mods3, row0, g1, w_in, q_g, k_g, rope_tabs, *, tm, want_q=True, want_aux=True):
    bx, lx, d = h.shape
    tn = KV_WIDTH
    nq = ATT_WIDTH // tn
    rope = rope_tabs is not None
    nqk = ATT_WIDTH + KV_WIDTH
    n_rest = w_in.shape[1] - nqk
    assert nqk % n_rest == 0
    aux_w = n_rest - KV_WIDTH
    row = lambda b, i: (b, i, 0)

    in_specs = [
        pl.BlockSpec((None, tm, d), row),
        pl.BlockSpec((None, 1, d), _mod_spec(row0, 0)),
        pl.BlockSpec((None, 1, d), _mod_spec(row0, 1)),
        _const_spec((1, d)),
        pl.BlockSpec((d, nqk), lambda b, i: (0, 0), pipeline_mode=pl.Buffered(1)),
        pl.BlockSpec((d, n_rest), lambda b, i: (0, nqk // n_rest), pipeline_mode=pl.Buffered(1)),
        _const_spec((1, HEAD_DIM)),
        _const_spec((1, HEAD_DIM)),
    ]
    args = [h, mods3, mods3, g1.reshape(1, d), w_in, w_in, q_g.reshape(1, HEAD_DIM), k_g.reshape(1, HEAD_DIM)]
    if rope:
        in_specs += [pl.BlockSpec((tm, HEAD_DIM), lambda b, i: (i, 0))] * 2
        args += list(rope_tabs)

    out_specs, out_shape = [], []
    if want_q:
        out_specs.append(pl.BlockSpec((None, tm, ATT_WIDTH), row))
        out_shape.append(jax.ShapeDtypeStruct((bx, lx, ATT_WIDTH), BF16))
    for _ in range(2):
        out_specs.append(pl.BlockSpec((None, tm, KV_WIDTH), row))
        out_shape.append(jax.ShapeDtypeStruct((bx, lx, KV_WIDTH), BF16))
    if want_aux:
        out_specs.append(pl.BlockSpec((None, tm, aux_w), row))
        out_shape.append(jax.ShapeDtypeStruct((bx, lx, aux_w), F32))

    return pl.pallas_call(
        functools.partial(_inproj_kernel, tn=tn, nq=nq, rope=rope, want_q=want_q, want_aux=want_aux),
        grid=(bx, lx // tm),
        in_specs=in_specs,
        out_specs=out_specs,
        out_shape=out_shape,
        scratch_shapes=[pltpu.VMEM((tm, d), BF16)],
        name="inproj",
        compiler_params=_params(("arbitrary", "arbitrary")),
    )(*args)


def _transpose_bf16(x):
    eye = (lax.broadcasted_iota(jnp.int32, (HEAD_DIM, HEAD_DIM), 0)
           == lax.broadcasted_iota(jnp.int32, (HEAD_DIM, HEAD_DIM), 1)).astype(BF16)
    return lax.dot_general(eye, x, (((1,), (1,)), ((), ())), preferred_element_type=F32).astype(BF16)


def _scores(kch, q, mask):
    s = lax.dot_general(kch, q, (((1,), (1,)), ((), ())), preferred_element_type=F32)
    return s if mask is None else jnp.where(mask, s, NEG_INF)


def _softmax_step(state, s, vt):
    m_prev, acc = state
    m_next = jnp.maximum(m_prev, jnp.max(s, axis=0, keepdims=True))
    p = jnp.exp2(s - m_next).astype(BF16)
    alpha = jnp.exp2(m_prev - m_next)
    acc = alpha * acc + jnp.dot(vt, p, preferred_element_type=F32)
    return m_next, acc


def _attn_kernel(*refs, mode, tq, nsub, tk, seq, has_sink):
    it = iter(refs)
    q_ref = next(it)
    k_ref = v_ref = None
    if mode != "ctx":
        k_ref, v_ref = next(it), next(it)
    kc_ref, vc_ref = next(it), next(it)
    sink_ref = next(it) if has_sink else None
    o_ref = next(it)
    vt_ref = next(it) if mode != "ctx" else None
    vct_ref = next(it)

    kvh = pl.program_id(1)
    step = pl.program_id(2)

    @pl.when(step == 0)
    def _():
        vct_ref[:HEAD_DIM, :] = _transpose_bf16(vc_ref[...])
        vct_ref[HEAD_DIM:, :] = jnp.ones((VT_ROWS - HEAD_DIM, vct_ref.shape[1]), BF16)
        if mode != "ctx":
            vt_ref[HEAD_DIM:, :] = jnp.ones((VT_ROWS - HEAD_DIM, seq), BF16)
            for c in range(0, seq, tk):
                vt_ref[:HEAD_DIM, c:c + tk] = _transpose_bf16(v_ref[c:c + tk, :])

    n = GROUPS * tq
    if mode == "window":
        band0 = (lax.broadcasted_iota(jnp.int32, (tk, n), 0)
                 - (lax.broadcasted_iota(jnp.int32, (tk, n), 1) & (tq - 1)) + WINDOW)
        interior_mask = (band0 - WINDOW).astype(jnp.uint32) <= 2 * WINDOW

    items = []
    for sub in range(nsub):
        start = (step * nsub + sub) * tq
        rows = slice(sub * tq, (sub + 1) * tq)
        q = jnp.concatenate([q_ref[rows, g * HEAD_DIM:(g + 1) * HEAD_DIM] for g in range(GROUPS)], axis=0)
        if mode == "dense":
            for c in range(0, seq, tk):
                items.append((sub, q, None, (lambda c=c: k_ref[c:c + tk, :]), (lambda c=c: vt_ref[:, c:c + tk])))
        elif mode == "window":
            ks = pl.multiple_of(jnp.clip(start - WINDOW, 0, seq - tk), WINDOW)
            if 0 < sub < nsub - 1:
                mask = interior_mask
            else:
                mask = (band0 + (ks - start)).astype(jnp.uint32) <= 2 * WINDOW
            items.append((sub, q, mask, (lambda ks=ks: k_ref[pl.ds(ks, tk), :]),
                          (lambda ks=ks: vt_ref[:, pl.ds(ks, tk)])))
        items.append((sub, q, None, (lambda: kc_ref[...]), (lambda: vct_ref[...])))

    per_sub = len(items) // nsub
    items = [items[sub * per_sub + c] for c in range(per_sub) for sub in range(nsub)]

    states = [(jnp.full((1, n), NEG_INF, F32), jnp.zeros((VT_ROWS, n), F32)) for _ in range(nsub)]
    score = lambda item: _scores(item[3](), item[1], item[2])
    pending = [score(item) for item in items[:SCORE_LOOKAHEAD]]
    for idx, (sub, _, _, _, vt_fn) in enumerate(items):
        s_cur = pending.pop(0)
        if idx + SCORE_LOOKAHEAD < len(items):
            pending.append(score(items[idx + SCORE_LOOKAHEAD]))
        states[sub] = _softmax_step(states[sub], s_cur, vt_fn())

    for sub, (m, acc) in enumerate(states):
        l = acc[HEAD_DIM:HEAD_DIM + 1, :]
        acc = acc[:HEAD_DIM, :]
        if has_sink:
            sink = jnp.concatenate([jnp.full((1, tq), sink_ref[kvh * GROUPS + g] * LOG2E, F32)
                                    for g in range(GROUPS)], axis=1)
            l = l + jnp.exp2(sink - m)
        out = acc / l
        for g in range(GROUPS):
            o_ref[sub * tq:(sub + 1) * tq, g * HEAD_DIM:(g + 1) * HEAD_DIM] = (
                out[:, g * tq:(g + 1) * tq].T.astype(BF16))


def _attention(q, k, v, kc, vc, sink, *, mode, tq, nsub, tk):
    b, lq, _ = q.shape
    lc = kc.shape[1]
    seq = k.shape[1] if k is not None else 0
    gw = GROUPS * HEAD_DIM
    tstep = tq * nsub
    in_specs = [pl.BlockSpec((None, tstep, gw), lambda bb, h, i: (bb, i, h))]
    args = [q]
    scratch = []
    if mode != "ctx":
        in_specs += [pl.BlockSpec((None, seq, HEAD_DIM), lambda bb, h, i: (bb, 0, h))] * 2
        args += [k, v]
        scratch.append(pltpu.VMEM((VT_ROWS, seq), BF16))
    in_specs += [pl.BlockSpec((None, lc, HEAD_DIM), lambda bb, h, i: (bb, 0, h))] * 2
    args += [kc, vc]
    scratch.append(pltpu.VMEM((VT_ROWS, lc), BF16))
    has_sink = sink is not None
    if has_sink:
        in_specs.append(pl.BlockSpec(memory_space=pltpu.SMEM))
        args.append(sink)
    return pl.pallas_call(
        functools.partial(_attn_kernel, mode=mode, tq=tq, nsub=nsub, tk=tk, seq=seq, has_sink=has_sink),
        grid=(b, KV_HEADS, lq // tstep),
        in_specs=in_specs,
        out_specs=pl.BlockSpec((None, tstep, gw), lambda bb, h, i: (bb, i, h)),
        out_shape=jax.ShapeDtypeStruct((b, lq, ATT_WIDTH), BF16),
        scratch_shapes=scratch,
        name="attn_" + mode,
        compiler_params=_params(("arbitrary", "arbitrary", "arbitrary")),
    )(*args)


def _conv_mixer(aux_ref, prev_ref, next_ref, w_ref, row0, seq_len):
    tm, w = aux_ref.shape[0], aux_ref.shape[1] // 3
    z = aux_ref[:, w:2 * w] * aux_ref[:, 2 * w:]
    z_prev = prev_ref[HALO - 1:HALO, w:2 * w] * prev_ref[HALO - 1:HALO, 2 * w:]
    z_next = next_ref[0:1, w:2 * w] * next_ref[0:1, 2 * w:]
    row = lax.broadcasted_iota(jnp.int32, (tm, w), 0)
    pos = (row0 + row) & (seq_len - 1)
    down = jnp.where(row == 0, z_prev, pltpu.roll(z, 1, 0))
    down = jnp.where(pos == 0, 0.0, down)
    up = jnp.where(row == tm - 1, z_next, pltpu.roll(z, tm - 1, 0))
    up = jnp.where(pos == seq_len - 1, 0.0, up)
    cw = w_ref[...]
    return aux_ref[:, :w] * (cw[0:1, :] * down + cw[1:2, :] * z + cw[2:3, :] * up)


def _pool_mixer(u_ref, prev_ref, next_ref, w_ref, s_ref, row0, seq_len):
    tm = u_ref.shape[0]
    at_start = (row0 & (seq_len - 1)) == 0
    at_end = ((row0 + tm) & (seq_len - 1)) == 0
    pos = (row0 & (seq_len - 1)) + lax.broadcasted_iota(jnp.int32, (tm, LANES), 0)
    outs = []
    for g, win in enumerate(POOL_WINDOWS):
        cols = slice(g * LANES, (g + 1) * LANES)
        u = u_ref[:, cols]
        ext = jnp.concatenate([jnp.where(at_start, 0.0, prev_ref[:, cols]), u,
                               jnp.where(at_end, 0.0, next_ref[:, cols])], axis=0)
        n = ext.shape[0]
        half = win // 2
        fwd = ext
        bwd = pltpu.roll(ext, 1, 0)
        k = 1
        while k < half:
            fwd = fwd + pltpu.roll(fwd, n - k, 0)
            bwd = bwd + pltpu.roll(bwd, k, 0)
            k *= 2
        cnt = jnp.minimum(pos + half, seq_len) - jnp.maximum(pos - half, 0)
        pooled = (fwd + bwd)[HALO:HALO + tm, :] / cnt.astype(F32) - u
        mixed = jnp.dot(pooled.astype(BF16), w_ref[g], preferred_element_type=F32)
        outs.append(mixed * s_ref[:, cols])
    return jnp.concatenate(outs, axis=1)


def _outproj_kernel(*refs, mixer, rows, seq_len):
    it = iter(refs)
    a_ref, aux_ref, prev_ref, next_ref = (next(it) for _ in range(4))
    mix_refs = [next(it) for _ in range(1 if mixer == "conv" else 2)]
    wa_ref, wx_ref, h_ref, gate_ref, shift_ref, scale_ref, g2_ref, o_ref, hn_ref, x_ref = it

    row0 = pl.program_id(1) * a_ref.shape[0]
    mix = _conv_mixer if mixer == "conv" else _pool_mixer
    gate = gate_ref[...]
    gs = g2_ref[...] * (1.0 + scale_ref[...])
    shift = shift_ref[...]
    for r in range(0, a_ref.shape[0], rows):
        rs = slice(r, r + rows)
        o_ref[rs, :] = h_ref[rs, :] + gate * jnp.dot(a_ref[rs, :], wa_ref[...], preferred_element_type=F32)
        if r == 0:
            x_ref[...] = mix(aux_ref, prev_ref, next_ref, *mix_refs, row0, seq_len).astype(BF16)
        o_ref[rs, :] += gate * jnp.dot(x_ref[rs, :], wx_ref[...], preferred_element_type=F32)
        _norm_to(hn_ref, o_ref.at[rs, :], gs, shift, row0=r)


def _outproj(attn, aux, mixer, mix_params, seq_len, w_out, h, mods3, row0, g2, *, tm):
    bx, lx, d = h.shape
    wa, aux_w = attn.shape[2], aux.shape[2]
    wx = d - wa
    row = lambda b, i: (b, i, 0)
    halo_blocks = tm // HALO
    mix_specs = [_const_spec(p.shape) for p in mix_params]
    return pl.pallas_call(
        functools.partial(_outproj_kernel, mixer=mixer, rows=tm // 2, seq_len=seq_len),
        grid=(bx, lx // tm),
        in_specs=[pl.BlockSpec((None, tm, wa), row),
                  pl.BlockSpec((None, tm, aux_w), row),
                  pl.BlockSpec((None, HALO, aux_w), lambda b, i: (b, jnp.maximum(i * halo_blocks - 1, 0), 0)),
                  pl.BlockSpec((None, HALO, aux_w),
                               lambda b, i: (b, jnp.minimum((i + 1) * halo_blocks, lx // HALO - 1), 0)),
                  *mix_specs,
                  pl.BlockSpec((wa, d), lambda b, i: (0, 0), pipeline_mode=pl.Buffered(1)),
                  pl.BlockSpec((wx, d), lambda b, i: (wa // wx, 0), pipeline_mode=pl.Buffered(1)),
                  pl.BlockSpec((None, tm, d), row),
                  pl.BlockSpec((None, 1, d), _mod_spec(row0, 2)),
                  pl.BlockSpec((None, 1, d), _mod_spec(row0, 3)),
                  pl.BlockSpec((None, 1, d), _mod_spec(row0, 4)),
                  _const_spec((1, d))],
        out_specs=[pl.BlockSpec((None, tm, d), row), pl.BlockSpec((None, tm, d), row)],
        out_shape=[jax.ShapeDtypeStruct((bx, lx, d), F32), jax.ShapeDtypeStruct((bx, lx, d), BF16)],
        scratch_shapes=[pltpu.VMEM((tm, wx), BF16)],
        name="outproj_" + mixer,
        compiler_params=_params(("arbitrary", "arbitrary")),
    )(attn, aux, aux, aux, *mix_params, w_out, w_out, h, mods3, mods3, mods3, g2.reshape(1, d))


def _ffn_kernel(*refs, final, th, rows, ncols):
    it = iter(refs)
    hn_ref, h_hbm, gate_ref, wg_hbm, wu_hbm, wd_hbm = (next(it) for _ in range(6))
    fg_ref = next(it) if final else None
    o_ref, wg_buf, wu_buf, wd_buf, h_buf, w_sem, h_sem = (next(it) for _ in range(7))

    b, i = pl.program_id(0), pl.program_id(1)
    tm, d = o_ref.shape
    n_hid = wg_hbm.shape[1] // th
    tile = b * pl.num_programs(1) + i
    n_steps = pl.num_programs(0) * pl.num_programs(1) * n_hid
    step0 = tile * n_hid

    def w_copies(j, slot):
        col = j * th if isinstance(j, int) else pl.multiple_of(j * th, th)
        return (pltpu.make_async_copy(wg_hbm.at[:, pl.ds(col, th)], wg_buf.at[slot], w_sem.at[0, slot]),
                pltpu.make_async_copy(wu_hbm.at[:, pl.ds(col, th)], wu_buf.at[slot], w_sem.at[1, slot]),
                pltpu.make_async_copy(wd_hbm.at[pl.ds(col, th), :], wd_buf.at[slot], w_sem.at[2, slot]))

    h_copy = pltpu.make_async_copy(h_hbm.at[pl.ds(pl.multiple_of(tile * tm, tm), tm), :], h_buf, h_sem)
    h_copy.start()

    @pl.when(tile == 0)
    def _():
        for cp in w_copies(0, 0):
            cp.start()

    gate = gate_ref[...]

    def hidden_step(j, first=False, last=False):
        step = step0 + j
        slot = lax.rem(step, 2)
        for cp in w_copies(j, slot):
            cp.wait()

        @pl.when(step + 1 < n_steps)
        def _():
            nxt = jnp.where(j + 1 < n_hid, j + 1, 0)
            for cp in w_copies(nxt, 1 - slot):
                cp.start()

        wg, wu, wd = wg_buf.at[slot], wu_buf.at[slot], wd_buf.at[slot]
        for r in range(0, tm, rows):
            rs = slice(r, r + rows)
            hn = hn_ref[rs, :]
            gt = jnp.dot(hn, wg[...], preferred_element_type=F32)
            up = jnp.dot(hn, wu[...], preferred_element_type=F32)
            act = (gt * jax.nn.sigmoid(gt) * up).astype(BF16)
            if last and r == 0:
                h_copy.wait()
            for c in range(0, d, ncols):
                cs = slice(c, c + ncols)
                dn = gate[:, cs] * jnp.dot(act, wd[:, cs], preferred_element_type=F32)
                if first:
                    o_ref[rs, cs] = dn
                elif last:
                    o_ref[rs, cs] += dn + h_buf[rs, cs]
                else:
                    o_ref[rs, cs] += dn

    hidden_step(0, first=True)

    def loop_body(j, carry):
        hidden_step(j)
        return carry

    lax.fori_loop(1, n_hid - 1, loop_body, 0)
    hidden_step(n_hid - 1, last=True)

    if final:
        fg = fg_ref[...]
        for r in range(0, tm, NORM_ROWS):
            x = o_ref[r:r + NORM_ROWS, :]
            ms = jnp.mean(x * x, axis=-1, keepdims=True)
            o_ref[r:r + NORM_ROWS, :] = x * lax.rsqrt(ms + EPS) * fg


def _ffn(hn, h, mods3, row0, w_gate, w_up, w_down, final_g, *, tm, th, rows, ncols):
    bx, lx, d = h.shape
    final = final_g is not None
    row = lambda b, i: (b, i, 0)
    any_spec = pl.BlockSpec(memory_space=pl.ANY)
    in_specs = [pl.BlockSpec((None, tm, d), row), any_spec,
                pl.BlockSpec((None, 1, d), _mod_spec(row0, 5)),
                any_spec, any_spec, any_spec]
    args = [hn, h.reshape(bx * lx, d), mods3, w_gate, w_up, w_down]
    if final:
        in_specs.append(_const_spec((1, d)))
        args.append(final_g.reshape(1, d))
    return pl.pallas_call(
        functools.partial(_ffn_kernel, final=final, th=th, rows=rows, ncols=ncols),
        grid=(bx, lx // tm),
        in_specs=in_specs,
        out_specs=pl.BlockSpec((tm, d), lambda b, i: (b * (lx // tm) + i, 0)),
        out_shape=jax.ShapeDtypeStruct((bx * lx, d), F32),
        scratch_shapes=[pltpu.VMEM((2, d, th), BF16), pltpu.VMEM((2, d, th), BF16), pltpu.VMEM((2, th, d), BF16),
                        pltpu.VMEM((tm, d), F32), pltpu.SemaphoreType.DMA((3, 2)), pltpu.SemaphoreType.DMA(())],
        name="ffn",
        compiler_params=_params(("arbitrary", "arbitrary")),
    )(*args).reshape(bx, lx, d)


def _rope_tables(n):
    pos = np.arange(n)
    inv = np.power(ROPE_THETA, -np.arange(0, AXIS_DIM, 2, dtype=np.float64) / AXIS_DIM)
    ang_r = (pos // GRID_W)[:, None] * inv
    ang_c = (pos % GRID_W)[:, None] * inv
    cos = np.concatenate([np.cos(ang_r), np.cos(ang_r), np.cos(ang_c), np.cos(ang_c)], axis=-1)
    sin = np.concatenate([-np.sin(ang_r), np.sin(ang_r), -np.sin(ang_c), np.sin(ang_c)], axis=-1)
    return jnp.asarray(cos, F32), jnp.asarray(sin, F32)


def kernel(x, c, ctx, c_ctx, l0_norm1_g, l0_w_mod, l0_b_mod, l0_w_in, l0_q_norm_g, l0_k_norm_g, l0_conv_w, l0_w_out, l0_norm2_g, l0_w_gate, l0_w_up, l0_w_down, l1_norm1_g, l1_w_mod, l1_b_mod, l1_w_in, l1_q_norm_g, l1_k_norm_g, l1_sink, l1_pool_w, l1_pool_scale, l1_w_out, l1_norm2_g, l1_w_gate, l1_w_up, l1_w_down, final_norm_g):
    b, s, d = x.shape
    lc = ctx.shape[1]
    rope = _rope_tables(s)
    c_all = jnp.zeros((MOD_ROWS, d), F32).at[:b].set(c).at[b].set(c_ctx)
    ctx_row = b
    bf = lambda w: w.astype(BF16)
    tm_in, tm_out = 512, 512
    ffn_tiles = dict(tm=1024, th=512, rows=512, ncols=512)

    mods = _modulation(c_all, l0_w_mod, l0_b_mod).reshape(MOD_ROWS, 1, N_MOD * d)
    w_in, w_out = bf(l0_w_in), bf(l0_w_out)
    q_g, k_g = l0_q_norm_g, l0_k_norm_g
    w_gate, w_up, w_down = bf(l0_w_gate), bf(l0_w_up), bf(l0_w_down)
    hc = ctx.reshape(1, b * lc, d)

    q, k, v, aux = _inproj(x, mods, 0, l0_norm1_g, w_in, q_g, k_g, rope, tm=tm_in)
    qc, kc, vc, auxc = _inproj(hc, mods, ctx_row, l0_norm1_g, w_in, q_g, k_g, None, tm=tm_in)
    qc, kc, vc = (t.reshape(b, lc, -1) for t in (qc, kc, vc))

    attn = _attention(q, k, v, kc, vc, None, mode="dense", tq=256, nsub=2, tk=512)
    h, hn = _outproj(attn, aux, "conv", (l0_conv_w,), s, w_out, x, mods, 0, l0_norm2_g, tm=tm_out)
    h = _ffn(hn, h, mods, 0, w_gate, w_up, w_down, None, **ffn_tiles)

    attn_c = _attention(qc, None, None, kc, vc, None, mode="ctx", tq=lc, nsub=1, tk=lc)
    hc, hcn = _outproj(attn_c.reshape(1, b * lc, -1), auxc, "conv", (l0_conv_w,), lc, w_out, hc, mods, ctx_row,
                       l0_norm2_g, tm=tm_out)
    hc = _ffn(hcn, hc, mods, ctx_row, w_gate, w_up, w_down, None, **ffn_tiles)

    mods = _modulation(c_all, l1_w_mod, l1_b_mod).reshape(MOD_ROWS, 1, N_MOD * d)
    w_in, w_out = bf(l1_w_in), bf(l1_w_out)
    q_g, k_g = l1_q_norm_g, l1_k_norm_g
    w_gate, w_up, w_down = bf(l1_w_gate), bf(l1_w_up), bf(l1_w_down)

    q, k, v, u = _inproj(h, mods, 0, l1_norm1_g, w_in, q_g, k_g, rope, tm=tm_in)
    kc, vc = _inproj(hc, mods, ctx_row, l1_norm1_g, w_in, q_g, k_g, None, tm=tm_in,
                     want_q=False, want_aux=False)
    kc, vc = kc.reshape(b, lc, -1), vc.reshape(b, lc, -1)

    attn = _attention(q, k, v, kc, vc, l1_sink, mode="window", tq=256, nsub=8, tk=512)
    pool_params = (bf(l1_pool_w), l1_pool_scale.reshape(1, -1))
    h, hn = _outproj(attn, u, "pool", pool_params, s, w_out, h, mods, 0, l1_norm2_g, tm=tm_out)
    return _ffn(hn, h, mods, 0, w_gate, w_up, w_down, final_norm_g, **ffn_tiles)
```

```python
import functools
import math

import jax
import jax.numpy as jnp
import numpy as np
from jax import lax
from jax.experimental import pallas as pl
from jax.experimental.pallas import tpu as pltpu

F32 = jnp.float32
BF16 = jnp.bfloat16

LANES = 128
HEAD_DIM = 128
AXIS_DIM = HEAD_DIM // 2
Q_HEADS = 12
KV_HEADS = 4
GROUPS = Q_HEADS // KV_HEADS
ATT_WIDTH = Q_HEADS * HEAD_DIM
KV_WIDTH = KV_HEADS * HEAD_DIM
GRID_W = 64
WINDOW = 128
POOL_WINDOWS = (2, 4, 8, 16)
ROPE_THETA = 10000.0
N_MOD = 6
EPS = 1e-6
NEG_INF = -1e30
ATT_SCALE = HEAD_DIM ** -0.5
LOG2E = math.log2(math.e)
MOD_ROWS = 8
VMEM_LIMIT = 56 * 1024 * 1024
NORM_ROWS = 16
HALO = 8
SCORE_LOOKAHEAD = 2
VT_ROWS = HEAD_DIM + 16


def _params(sem):
    return pltpu.CompilerParams(dimension_semantics=sem, vmem_limit_bytes=VMEM_LIMIT)


def _mod_spec(row0, chunk):
    def imap(b, *_):
        return (b + row0, 0, chunk)
    return imap


def _const_spec(shape):
    zeros = (0,) * len(shape)
    return pl.BlockSpec(shape, lambda *_: zeros, pipeline_mode=pl.Buffered(1))


def _mod_kernel(c_ref, w_ref, b_ref, o_ref):
    c = c_ref[...]
    sc = c * jax.nn.sigmoid(c)
    o_ref[...] = jnp.dot(sc.astype(BF16), w_ref[...].astype(BF16),
                         preferred_element_type=F32) + b_ref[...]


def _modulation(c_all, w_mod, b_mod):
    d, n = w_mod.shape
    tn = 1024
    return pl.pallas_call(
        _mod_kernel,
        grid=(n // tn,),
        in_specs=[pl.BlockSpec((MOD_ROWS, d), lambda j: (0, 0)),
                  pl.BlockSpec((d, tn), lambda j: (0, j)),
                  pl.BlockSpec((1, tn), lambda j: (0, j))],
        out_specs=pl.BlockSpec((MOD_ROWS, tn), lambda j: (0, j)),
        out_shape=jax.ShapeDtypeStruct((MOD_ROWS, n), F32),
        name="modulation",
        compiler_params=_params(("arbitrary",)),
    )(c_all, w_mod, b_mod.reshape(1, n))


def _rms_modulate_rows(x, gs, shift):
    ms = jnp.mean(x * x, axis=-1, keepdims=True)
    return x * lax.rsqrt(ms + EPS) * gs + shift


def _norm_to(dst_ref, src, gs, shift, row0=0):
    rows = src.shape[0]
    for r in range(0, rows, NORM_ROWS):
        x = src[r:r + NORM_ROWS, :]
        dst_ref[row0 + r:row0 + r + NORM_ROWS, :] = _rms_modulate_rows(x, gs, shift).astype(BF16)


def _head_norm_rope(y, g, cos, sin):
    lane = lax.broadcasted_iota(jnp.int32, (1, HEAD_DIM), 1)
    in_x1 = (lane & (AXIS_DIM // 2)) == 0
    outs = []
    for hh in range(y.shape[1] // HEAD_DIM):
        t = y[:, hh * HEAD_DIM:(hh + 1) * HEAD_DIM]
        ms = jnp.mean(t * t, axis=-1, keepdims=True)
        t = t * lax.rsqrt(ms + EPS) * g
        if cos is not None:
            partner = jnp.where(in_x1, pltpu.roll(t, HEAD_DIM - AXIS_DIM // 2, 1), pltpu.roll(t, AXIS_DIM // 2, 1))
            t = t * cos + partner * sin
        outs.append(t)
    return jnp.concatenate(outs, axis=1)


def _inproj_kernel(*refs, tn, nq, rows, rope, want_q, want_aux):
    it = iter(refs)
    h_ref, shift_ref, scale_ref, g1_ref, wqk_ref, wr_ref, qg_ref, kg_ref = (next(it) for _ in range(8))
    cos_ref = sin_ref = None
    if rope:
        cos_ref, sin_ref = next(it), next(it)
    q_ref = next(it) if want_q else None
    k_ref, v_ref = next(it), next(it)
    aux_ref = next(it) if want_aux else None
    xn_ref = next(it)

    gs = g1_ref[...] * (1.0 + scale_ref[...])
    shift = shift_ref[...]
    qg = qg_ref[...] * (ATT_SCALE * LOG2E)
    kg = kg_ref[...]

    for r in range(0, h_ref.shape[0], rows):
        rs = slice(r, r + rows)
        _norm_to(xn_ref, h_ref.at[rs, :], gs, shift, row0=r)
        cos = cos_ref[rs, :] if rope else None
        sin = sin_ref[rs, :] if rope else None

        def tile(w_ref, j):
            return jnp.dot(xn_ref[rs, :], w_ref[:, j * tn:(j + 1) * tn], preferred_element_type=F32)

        if want_q:
            for j in range(nq):
                q_ref[rs, j * tn:(j + 1) * tn] = _head_norm_rope(tile(wqk_ref, j), qg, cos, sin).astype(BF16)
        k_ref[rs, :] = _head_norm_rope(tile(wqk_ref, nq), kg, cos, sin).astype(BF16)
        v_ref[rs, :] = tile(wr_ref, 0).astype(BF16)
        if want_aux:
            for j in range(aux_ref.shape[1] // tn):
                aux_ref[rs, j * tn:(j + 1) * tn] = tile(wr_ref, 1 + j)


def _inproj(h, mods3, row0, g1, w_in, q_g, k_g, rope_tabs, *, tm, want_q=True, want_aux=True):
    bx, lx, d = h.shape
    tn = KV_WIDTH
    nq = ATT_WIDTH // tn
    rope = rope_tabs is not None
    nqk = ATT_WIDTH + KV_WIDTH
    n_rest = w_in.shape[1] - nqk
    assert nqk % n_rest == 0
    aux_w = n_rest - KV_WIDTH
    row = lambda b, i: (b, i, 0)

    in_specs = [
        pl.BlockSpec((None, tm, d), row),
        pl.BlockSpec((None, 1, d), _mod_spec(row0, 0)),
        pl.BlockSpec((None, 1, d), _mod_spec(row0, 1)),
        _const_spec((1, d)),
        pl.BlockSpec((d, nqk), lambda b, i: (0, 0), pipeline_mode=pl.Buffered(1)),
        pl.BlockSpec((d, n_rest), lambda b, i: (0, nqk // n_rest), pipeline_mode=pl.Buffered(1)),
        _const_spec((1, HEAD_DIM)),
        _const_spec((1, HEAD_DIM)),
    ]
    args = [h, mods3, mods3, g1.reshape(1, d), w_in, w_in, q_g.reshape(1, HEAD_DIM), k_g.reshape(1, HEAD_DIM)]
    if rope:
        in_specs += [pl.BlockSpec((tm, HEAD_DIM), lambda b, i: (i, 0))] * 2
        args += list(rope_tabs)

    out_specs, out_shape = [], []
    if want_q:
        out_specs.append(pl.BlockSpec((None, tm, ATT_WIDTH), row))
        out_shape.append(jax.ShapeDtypeStruct((bx, lx, ATT_WIDTH), BF16))
    for _ in range(2):
        out_specs.append(pl.BlockSpec((None, tm, KV_WIDTH), row))
        out_shape.append(jax.ShapeDtypeStruct((bx, lx, KV_WIDTH), BF16))
    if want_aux:
        out_specs.append(pl.BlockSpec((None, tm, aux_w), row))
        out_shape.append(jax.ShapeDtypeStruct((bx, lx, aux_w), F32))

    return pl.pallas_call(
        functools.partial(_inproj_kernel, tn=tn, nq=nq, rows=tm // 2, rope=rope, want_q=want_q,
                          want_aux=want_aux),
        grid=(bx, lx // tm),
        in_specs=in_specs,
        out_specs=out_specs,
        out_shape=out_shape,
        scratch_shapes=[pltpu.VMEM((tm, d), BF16)],
        name="inproj",
        compiler_params=_params(("arbitrary", "arbitrary")),
    )(*args)


def _transpose_bf16(x):
    eye = (lax.broadcasted_iota(jnp.int32, (HEAD_DIM, HEAD_DIM), 0)
           == lax.broadcasted_iota(jnp.int32, (HEAD_DIM, HEAD_DIM), 1)).astype(BF16)
    return lax.dot_general(eye, x, (((1,), (1,)), ((), ())), preferred_element_type=F32).astype(BF16)


def _scores(kch, q, mask):
    s = lax.dot_general(kch, q, (((1,), (1,)), ((), ())), preferred_element_type=F32)
    return s if mask is None else jnp.where(mask, s, NEG_INF)


def _softmax_step(state, s, vt):
    m_prev, acc = state
    m_next = jnp.maximum(m_prev, jnp.max(s, axis=0, keepdims=True))
    p = jnp.exp2(s - m_next).astype(BF16)
    alpha = jnp.exp2(m_prev - m_next)
    acc = alpha * acc + jnp.dot(vt, p, preferred_element_type=F32)
    return m_next, acc


def _attn_kernel(*refs, mode, tq, nsub, tk, seq, has_sink):
    it = iter(refs)
    q_ref = next(it)
    k_ref = v_ref = None
    if mode != "ctx":
        k_ref, v_ref = next(it), next(it)
    kc_ref, vc_ref = next(it), next(it)
    sink_ref = next(it) if has_sink else None
    o_ref = next(it)
    vt_ref = next(it) if mode != "ctx" else None
    vct_ref = next(it)

    kvh = pl.program_id(1)
    step = pl.program_id(2)

    @pl.when(step == 0)
    def _():
        vct_ref[:HEAD_DIM, :] = _transpose_bf16(vc_ref[...])
        vct_ref[HEAD_DIM:, :] = jnp.ones((VT_ROWS - HEAD_DIM, vct_ref.shape[1]), BF16)
        if mode != "ctx":
            vt_ref[HEAD_DIM:, :] = jnp.ones((VT_ROWS - HEAD_DIM, seq), BF16)
            for c in range(0, seq, tk):
                vt_ref[:HEAD_DIM, c:c + tk] = _transpose_bf16(v_ref[c:c + tk, :])

    n = GROUPS * tq
    if mode == "window":
        band0 = (lax.broadcasted_iota(jnp.int32, (tk, n), 0)
                 - (lax.broadcasted_iota(jnp.int32, (tk, n), 1) & (tq - 1)) + WINDOW)
        interior_mask = (band0 - WINDOW).astype(jnp.uint32) <= 2 * WINDOW

    items = []
    for sub in range(nsub):
        start = (step * nsub + sub) * tq
        rows = slice(sub * tq, (sub + 1) * tq)
        q = jnp.concatenate([q_ref[rows, g * HEAD_DIM:(g + 1) * HEAD_DIM] for g in range(GROUPS)], axis=0)
        if mode == "dense":
            for c in range(0, seq, tk):
                items.append((sub, q, None, (lambda c=c: k_ref[c:c + tk, :]), (lambda c=c: vt_ref[:, c:c + tk])))
        elif mode == "window":
            ks = pl.multiple_of(jnp.clip(start - WINDOW, 0, seq - tk), WINDOW)
            if 0 < sub < nsub - 1:
                mask = interior_mask
            else:
                mask = (band0 + (ks - start)).astype(jnp.uint32) <= 2 * WINDOW
            items.append((sub, q, mask, (lambda ks=ks: k_ref[pl.ds(ks, tk), :]),
                          (lambda ks=ks: vt_ref[:, pl.ds(ks, tk)])))
        items.append((sub, q, None, (lambda: kc_ref[...]), (lambda: vct_ref[...])))

    per_sub = len(items) // nsub
    items = [items[sub * per_sub + c] for c in range(per_sub) for sub in range(nsub)]

    states = [(jnp.full((1, n), NEG_INF, F32), jnp.zeros((VT_ROWS, n), F32)) for _ in range(nsub)]
    score = lambda item: _scores(item[3](), item[1], item[2])
    pending = [score(item) for item in items[:SCORE_LOOKAHEAD]]
    for idx, (sub, _, _, _, vt_fn) in enumerate(items):
        s_cur = pending.pop(0)
        if idx + SCORE_LOOKAHEAD < len(items):
            pending.append(score(items[idx + SCORE_LOOKAHEAD]))
        states[sub] = _softmax_step(states[sub], s_cur, vt_fn())

    for sub, (m, acc) in enumerate(states):
        l = acc[HEAD_DIM:HEAD_DIM + 1, :]
        acc = acc[:HEAD_DIM, :]
        if has_sink:
            sink = jnp.concatenate([jnp.full((1, tq), sink_ref[kvh * GROUPS + g] * LOG2E, F32)
                                    for g in range(GROUPS)], axis=1)
            l = l + jnp.exp2(sink - m)
        out = acc / l
        for g in range(GROUPS):
            o_ref[sub * tq:(sub + 1) * tq, g * HEAD_DIM:(g + 1) * HEAD_DIM] = (
                out[:, g * tq:(g + 1) * tq].T.astype(BF16))


def _attention(q, k, v, kc, vc, sink, *, mode, tq, nsub, tk):
    b, lq, _ = q.shape
    lc = kc.shape[1]
    seq = k.shape[1] if k is not None else 0
    gw = GROUPS * HEAD_DIM
    tstep = tq * nsub
    in_specs = [pl.BlockSpec((None, tstep, gw), lambda bb, h, i: (bb, i, h))]
    args = [q]
    scratch = []
    if mode != "ctx":
        in_specs += [pl.BlockSpec((None, seq, HEAD_DIM), lambda bb, h, i: (bb, 0, h))] * 2
        args += [k, v]
        scratch.append(pltpu.VMEM((VT_ROWS, seq), BF16))
    in_specs += [pl.BlockSpec((None, lc, HEAD_DIM), lambda bb, h, i: (bb, 0, h))] * 2
    args += [kc, vc]
    scratch.append(pltpu.VMEM((VT_ROWS, lc), BF16))
    has_sink = sink is not None
    if has_sink:
        in_specs.append(pl.BlockSpec(memory_space=pltpu.SMEM))
        args.append(sink)
    return pl.pallas_call(
        functools.partial(_attn_kernel, mode=mode, tq=tq, nsub=nsub, tk=tk, seq=seq, has_sink=has_sink),
        grid=(b, KV_HEADS, lq // tstep),
        in_specs=in_specs,
        out_specs=pl.BlockSpec((None, tstep, gw), lambda bb, h, i: (bb, i, h)),
        out_shape=jax.ShapeDtypeStruct((b, lq, ATT_WIDTH), BF16),
        scratch_shapes=scratch,
        name="attn_" + mode,
        compiler_params=_params(("arbitrary", "arbitrary", "arbitrary")),
    )(*args)


def _conv_mixer(g, aux_ref, prev_ref, next_ref, w_ref, row0, seq_len):
    tm, w = aux_ref.shape[0], aux_ref.shape[1] // 3
    gb, gc, uu = (slice(k * w + g * LANES, k * w + (g + 1) * LANES) for k in range(3))
    z = aux_ref[:, gc] * aux_ref[:, uu]
    z_prev = prev_ref[HALO - 1:HALO, gc] * prev_ref[HALO - 1:HALO, uu]
    z_next = next_ref[0:1, gc] * next_ref[0:1, uu]
    row = lax.broadcasted_iota(jnp.int32, (tm, LANES), 0)
    pos = (row0 + row) & (seq_len - 1)
    down = jnp.where(row == 0, z_prev, pltpu.roll(z, 1, 0))
    down = jnp.where(pos == 0, 0.0, down)
    up = jnp.where(row == tm - 1, z_next, pltpu.roll(z, tm - 1, 0))
    up = jnp.where(pos == seq_len - 1, 0.0, up)
    cw = w_ref[:, g * LANES:(g + 1) * LANES]
    return aux_ref[:, gb] * (cw[0:1, :] * down + cw[1:2, :] * z + cw[2:3, :] * up)


def _pool_mixer(g, u_ref, prev_ref, next_ref, w_ref, s_ref, row0, seq_len):
    tm = u_ref.shape[0]
    at_start = (row0 & (seq_len - 1)) == 0
    at_end = ((row0 + tm) & (seq_len - 1)) == 0
    pos = (row0 & (seq_len - 1)) + lax.broadcasted_iota(jnp.int32, (tm, LANES), 0)
    cols = slice(g * LANES, (g + 1) * LANES)
    u = u_ref[:, cols]
    ext = jnp.concatenate([jnp.where(at_start, 0.0, prev_ref[:, cols]), u,
                           jnp.where(at_end, 0.0, next_ref[:, cols])], axis=0)
    n = ext.shape[0]
    half = POOL_WINDOWS[g] // 2
    fwd = ext
    bwd = pltpu.roll(ext, 1, 0)
    k = 1
    while k < half:
        fwd = fwd + pltpu.roll(fwd, n - k, 0)
        bwd = bwd + pltpu.roll(bwd, k, 0)
        k *= 2
    cnt = jnp.minimum(pos + half, seq_len) - jnp.maximum(pos - half, 0)
    pooled = (fwd + bwd)[HALO:HALO + tm, :] / cnt.astype(F32) - u
    mixed = jnp.dot(pooled.astype(BF16), w_ref[g], preferred_element_type=F32)
    return mixed * s_ref[:, cols]


def _outproj_kernel(*refs, mixer, rows, seq_len):
    it = iter(refs)
    a_ref, aux_ref, prev_ref, next_ref = (next(it) for _ in range(4))
    mix_refs = [next(it) for _ in range(1 if mixer == "conv" else 2)]
    wa_ref, wx_ref, h_ref, gate_ref, shift_ref, scale_ref, g2_ref, o_ref, hn_ref, x_ref = it

    row0 = pl.program_id(1) * a_ref.shape[0]
    mix = _conv_mixer if mixer == "conv" else _pool_mixer
    gate = gate_ref[...]
    gs = g2_ref[...] * (1.0 + scale_ref[...])
    shift = shift_ref[...]
    n_groups = x_ref.shape[1] // LANES
    ncols = o_ref.shape[1] // n_groups
    for r in range(0, a_ref.shape[0], rows):
        rs = slice(r, r + rows)
        for g in range(n_groups):
            cs = slice(g * ncols, (g + 1) * ncols)
            o_ref[rs, cs] = h_ref[rs, cs] + gate[:, cs] * jnp.dot(a_ref[rs, :], wa_ref[:, cs],
                                                               preferred_element_type=F32)
            if r == 0:
                x_ref[:, g * LANES:(g + 1) * LANES] = mix(g, aux_ref, prev_ref, next_ref, *mix_refs,
                                                          row0, seq_len).astype(BF16)
        o_ref[rs, :] += gate * jnp.dot(x_ref[rs, :], wx_ref[...], preferred_element_type=F32)
        _norm_to(hn_ref, o_ref.at[rs, :], gs, shift, row0=r)


def _outproj(attn, aux, mixer, mix_params, seq_len, w_out, h, mods3, row0, g2, *, tm):
    bx, lx, d = h.shape
    wa, aux_w = attn.shape[2], aux.shape[2]
    wx = d - wa
    row = lambda b, i: (b, i, 0)
    halo_blocks = tm // HALO
    mix_specs = [_const_spec(p.shape) for p in mix_params]
    return pl.pallas_call(
        functools.partial(_outproj_kernel, mixer=mixer, rows=tm // 2, seq_len=seq_len),
        grid=(bx, lx // tm),
        in_specs=[pl.BlockSpec((None, tm, wa), row),
                  pl.BlockSpec((None, tm, aux_w), row),
                  pl.BlockSpec((None, HALO, aux_w), lambda b, i: (b, jnp.maximum(i * halo_blocks - 1, 0), 0)),
                  pl.BlockSpec((None, HALO, aux_w),
                               lambda b, i: (b, jnp.minimum((i + 1) * halo_blocks, lx // HALO - 1), 0)),
                  *mix_specs,
                  pl.BlockSpec((wa, d), lambda b, i: (0, 0), pipeline_mode=pl.Buffered(1)),
                  pl.BlockSpec((wx, d), lambda b, i: (wa // wx, 0), pipeline_mode=pl.Buffered(1)),
                  pl.BlockSpec((None, tm, d), row),
                  pl.BlockSpec((None, 1, d), _mod_spec(row0, 2)),
                  pl.BlockSpec((None, 1, d), _mod_spec(row0, 3)),
                  pl.BlockSpec((None, 1, d), _mod_spec(row0, 4)),
                  _const_spec((1, d))],
        out_specs=[pl.BlockSpec((None, tm, d), row), pl.BlockSpec((None, tm, d), row)],
        out_shape=[jax.ShapeDtypeStruct((bx, lx, d), F32), jax.ShapeDtypeStruct((bx, lx, d), BF16)],
        scratch_shapes=[pltpu.VMEM((tm, wx), BF16)],
        name="outproj_" + mixer,
        compiler_params=_params(("arbitrary", "arbitrary")),
    )(attn, aux, aux, aux, *mix_params, w_out, w_out, h, mods3, mods3, mods3, g2.reshape(1, d))


def _ffn_kernel(*refs, final, th, rows, ncols):
    it = iter(refs)
    hn_ref, h_hbm, gate_ref, wg_hbm, wu_hbm, wd_hbm = (next(it) for _ in range(6))
    fg_ref = next(it) if final else None
    o_ref, wg_buf, wu_buf, wd_buf, h_buf, w_sem, h_sem = (next(it) for _ in range(7))

    b, i = pl.program_id(0), pl.program_id(1)
    tm, d = o_ref.shape
    n_hid = wg_hbm.shape[1] // th
    tile = b * pl.num_programs(1) + i
    n_steps = pl.num_programs(0) * pl.num_programs(1) * n_hid
    step0 = tile * n_hid

    def w_copies(j, slot):
        col = j * th if isinstance(j, int) else pl.multiple_of(j * th, th)
        return (pltpu.make_async_copy(wg_hbm.at[:, pl.ds(col, th)], wg_buf.at[slot], w_sem.at[0, slot]),
                pltpu.make_async_copy(wu_hbm.at[:, pl.ds(col, th)], wu_buf.at[slot], w_sem.at[1, slot]),
                pltpu.make_async_copy(wd_hbm.at[pl.ds(col, th), :], wd_buf.at[slot], w_sem.at[2, slot]))

    h_copy = pltpu.make_async_copy(h_hbm.at[pl.ds(pl.multiple_of(tile * tm, tm), tm), :], h_buf, h_sem)
    h_copy.start()

    @pl.when(tile == 0)
    def _():
        for cp in w_copies(0, 0):
            cp.start()

    gate = gate_ref[...]

    def hidden_step(j, first=False, last=False):
        step = step0 + j
        slot = lax.rem(step, 2)
        for cp in w_copies(j, slot):
            cp.wait()

        @pl.when(step + 1 < n_steps)
        def _():
            nxt = jnp.where(j + 1 < n_hid, j + 1, 0)
            for cp in w_copies(nxt, 1 - slot):
                cp.start()

        wg, wu, wd = wg_buf.at[slot], wu_buf.at[slot], wd_buf.at[slot]
        for r in range(0, tm, rows):
            rs = slice(r, r + rows)
            hn = hn_ref[rs, :]
            gt = jnp.dot(hn, wg[...], preferred_element_type=F32)
            up = jnp.dot(hn, wu[...], preferred_element_type=F32)
            act = (gt * jax.nn.sigmoid(gt) * up).astype(BF16)
            if last and r == 0:
                h_copy.wait()
            for c in range(0, d, ncols):
                cs = slice(c, c + ncols)
                dn = gate[:, cs] * jnp.dot(act, wd[:, cs], preferred_element_type=F32)
                if first:
                    o_ref[rs, cs] = dn
                elif last:
                    o_ref[rs, cs] += dn + h_buf[rs, cs]
                else:
                    o_ref[rs, cs] += dn

    hidden_step(0, first=True)

    def loop_body(j, carry):
        hidden_step(j)
        return carry

    lax.fori_loop(1, n_hid - 1, loop_body, 0)
    hidden_step(n_hid - 1, last=True)

    if final:
        fg = fg_ref[...]
        for r in range(0, tm, NORM_ROWS):
            x = o_ref[r:r + NORM_ROWS, :]
            ms = jnp.mean(x * x, axis=-1, keepdims=True)
            o_ref[r:r + NORM_ROWS, :] = x * lax.rsqrt(ms + EPS) * fg


def _ffn(hn, h, mods3, row0, w_gate, w_up, w_down, final_g, *, tm, th, rows, ncols):
    bx, lx, d = h.shape
    final = final_g is not None
    row = lambda b, i: (b, i, 0)
    any_spec = pl.BlockSpec(memory_space=pl.ANY)
    in_specs = [pl.BlockSpec((None, tm, d), row), any_spec,
                pl.BlockSpec((None, 1, d), _mod_spec(row0, 5)),
                any_spec, any_spec, any_spec]
    args = [hn, h.reshape(bx * lx, d), mods3, w_gate, w_up, w_down]
    if final:
        in_specs.append(_const_spec((1, d)))
        args.append(final_g.reshape(1, d))
    return pl.pallas_call(
        functools.partial(_ffn_kernel, final=final, th=th, rows=rows, ncols=ncols),
        grid=(bx, lx // tm),
        in_specs=in_specs,
        out_specs=pl.BlockSpec((tm, d), lambda b, i: (b * (lx // tm) + i, 0)),
        out_shape=jax.ShapeDtypeStruct((bx * lx, d), F32),
        scratch_shapes=[pltpu.VMEM((2, d, th), BF16), pltpu.VMEM((2, d, th), BF16), pltpu.VMEM((2, th, d), BF16),
                        pltpu.VMEM((tm, d), F32), pltpu.SemaphoreType.DMA((3, 2)), pltpu.SemaphoreType.DMA(())],
        name="ffn",
        compiler_params=_params(("arbitrary", "arbitrary")),
    )(*args).reshape(bx, lx, d)


def _rope_tables(n):
    pos = np.arange(n)
    inv = np.power(ROPE_THETA, -np.arange(0, AXIS_DIM, 2, dtype=np.float64) / AXIS_DIM)
    ang_r = (pos // GRID_W)[:, None] * inv
    ang_c = (pos % GRID_W)[:, None] * inv
    cos = np.concatenate([np.cos(ang_r), np.cos(ang_r), np.cos(ang_c), np.cos(ang_c)], axis=-1)
    sin = np.concatenate([-np.sin(ang_r), np.sin(ang_r), -np.sin(ang_c), np.sin(ang_c)], axis=-1)
    return jnp.asarray(cos, F32), jnp.asarray(sin, F32)


def kernel(x, c, ctx, c_ctx, l0_norm1_g, l0_w_mod, l0_b_mod, l0_w_in, l0_q_norm_g, l0_k_norm_g, l0_conv_w, l0_w_out, l0_norm2_g, l0_w_gate, l0_w_up, l0_w_down, l1_norm1_g, l1_w_mod, l1_b_mod, l1_w_in, l1_q_norm_g, l1_k_norm_g, l1_sink, l1_pool_w, l1_pool_scale, l1_w_out, l1_norm2_g, l1_w_gate, l1_w_up, l1_w_down, final_norm_g):
    b, s, d = x.shape
    lc = ctx.shape[1]
    rope = _rope_tables(s)
    c_all = jnp.zeros((MOD_ROWS, d), F32).at[:b].set(c).at[b].set(c_ctx)
    ctx_row = b
    bf = lambda w: w.astype(BF16)
    tm_in, tm_out = 512, 512
    ffn_tiles = dict(tm=1024, th=512, rows=512, ncols=512)

    mods = _modulation(c_all, l0_w_mod, l0_b_mod).reshape(MOD_ROWS, 1, N_MOD * d)
    w_in, w_out = bf(l0_w_in), bf(l0_w_out)
    q_g, k_g = l0_q_norm_g, l0_k_norm_g
    w_gate, w_up, w_down = bf(l0_w_gate), bf(l0_w_up), bf(l0_w_down)
    hc = ctx.reshape(1, b * lc, d)

    q, k, v, aux = _inproj(x, mods, 0, l0_norm1_g, w_in, q_g, k_g, rope, tm=tm_in)
    qc, kc, vc, auxc = _inproj(hc, mods, ctx_row, l0_norm1_g, w_in, q_g, k_g, None, tm=tm_in)
    qc, kc, vc = (t.reshape(b, lc, -1) for t in (qc, kc, vc))

    attn = _attention(q, k, v, kc, vc, None, mode="dense", tq=256, nsub=2, tk=512)
    h, hn = _outproj(attn, aux, "conv", (l0_conv_w,), s, w_out, x, mods, 0, l0_norm2_g, tm=tm_out)
    h = _ffn(hn, h, mods, 0, w_gate, w_up, w_down, None, **ffn_tiles)

    attn_c = _attention(qc, None, None, kc, vc, None, mode="ctx", tq=lc, nsub=1, tk=lc)
    hc, hcn = _outproj(attn_c.reshape(1, b * lc, -1), auxc, "conv", (l0_conv_w,), lc, w_out, hc, mods, ctx_row,
                       l0_norm2_g, tm=tm_out)
    hc = _ffn(hcn, hc, mods, ctx_row, w_gate, w_up, w_down, None, **ffn_tiles)

    mods = _modulation(c_all, l1_w_mod, l1_b_mod).reshape(MOD_ROWS, 1, N_MOD * d)
    w_in, w_out = bf(l1_w_in), bf(l1_w_out)
    q_g, k_g = l1_q_norm_g, l1_k_norm_g
    w_gate, w_up, w_down = bf(l1_w_gate), bf(l1_w_up), bf(l1_w_down)

    q, k, v, u = _inproj(h, mods, 0, l1_norm1_g, w_in, q_g, k_g, rope, tm=tm_in)
    kc, vc = _inproj(hc, mods, ctx_row, l1_norm1_g, w_in, q_g, k_g, None, tm=tm_in,
                     want_q=False, want_aux=False)
    kc, vc = kc.reshape(b, lc, -1), vc.reshape(b, lc, -1)

    attn = _attention(q, k, v, kc, vc, l1_sink, mode="window", tq=256, nsub=8, tk=512)
    pool_params = (bf(l1_pool_w), l1_pool_scale.reshape(1, -1))
    h, hn = _outproj(attn, u, "pool", pool_params, s, w_out, h, mods, 0, l1_norm2_g, tm=tm_out)
    return _ffn(hn, h, mods, 0, w_gate, w_up, w_down, final_norm_g, **ffn_tiles)
```

```python
import functools
import math

import jax
import jax.numpy as jnp
import numpy as np
from jax import lax
from jax.experimental import pallas as pl
from jax.experimental.pallas import tpu as pltpu

F32 = jnp.float32
BF16 = jnp.bfloat16

LANES = 128
HEAD_DIM = 128
AXIS_DIM = HEAD_DIM // 2
Q_HEADS = 12
KV_HEADS = 4
GROUPS = Q_HEADS // KV_HEADS
ATT_WIDTH = Q_HEADS * HEAD_DIM
KV_WIDTH = KV_HEADS * HEAD_DIM
GRID_W = 64
WINDOW = 128
POOL_WINDOWS = (2, 4, 8, 16)
ROPE_THETA = 10000.0
N_MOD = 6
EPS = 1e-6
NEG_INF = -1e30
ATT_SCALE = HEAD_DIM ** -0.5
LOG2E = math.log2(math.e)
MOD_ROWS = 8
VMEM_LIMIT = 56 * 1024 * 1024
IN_CHUNKS = 2
OUT_CHUNKS = 2
NORM_ROWS = 16
HALO = 8
SCORE_LOOKAHEAD = 2
VT_ROWS = HEAD_DIM + 16


def _params(sem):
    return pltpu.CompilerParams(dimension_semantics=sem, vmem_limit_bytes=VMEM_LIMIT)


def _mod_spec(row0, chunk):
    def imap(b, *_):
        return (b + row0, 0, chunk)
    return imap


def _const_spec(shape):
    zeros = (0,) * len(shape)
    return pl.BlockSpec(shape, lambda *_: zeros, pipeline_mode=pl.Buffered(1))


def _mod_kernel(c_ref, w_ref, b_ref, o_ref):
    c = c_ref[...]
    sc = c * jax.nn.sigmoid(c)
    o_ref[...] = jnp.dot(sc.astype(BF16), w_ref[...].astype(BF16),
                         preferred_element_type=F32) + b_ref[...]


def _modulation(c_all, w_mod, b_mod):
    d, n = w_mod.shape
    tn = 512
    return pl.pallas_call(
        _mod_kernel,
        grid=(n // tn,),
        in_specs=[pl.BlockSpec((MOD_ROWS, d), lambda j: (0, 0)),
                  pl.BlockSpec((d, tn), lambda j: (0, j)),
                  pl.BlockSpec((1, tn), lambda j: (0, j))],
        out_specs=pl.BlockSpec((MOD_ROWS, tn), lambda j: (0, j)),
        out_shape=jax.ShapeDtypeStruct((MOD_ROWS, n), F32),
        name="modulation",
        compiler_params=_params(("arbitrary",)),
    )(c_all, w_mod, b_mod.reshape(1, n))


def _rms_modulate_rows(x, gs, shift):
    ms = jnp.mean(x * x, axis=-1, keepdims=True)
    return x * lax.rsqrt(ms + EPS) * gs + shift


def _row_chunks(tm, n, unit=32):
    base, extra = divmod(tm // unit, n)
    return tuple((base + (1 if k < extra else 0)) * unit for k in range(n))


def _chunk_starts(sizes):
    starts = np.cumsum((0,) + tuple(sizes[:-1]))
    return [(int(s), int(n)) for s, n in zip(starts, sizes)]


def _norm_to(dst_ref, src, gs, shift, row0=0):
    rows = src.shape[0]
    for r in range(0, rows, NORM_ROWS):
        x = src[r:r + NORM_ROWS, :]
        dst_ref[row0 + r:row0 + r + NORM_ROWS, :] = _rms_modulate_rows(x, gs, shift).astype(BF16)


def _head_norm_rope(y, g, cos, sin):
    lane = lax.broadcasted_iota(jnp.int32, (1, HEAD_DIM), 1)
    in_x1 = (lane & (AXIS_DIM // 2)) == 0
    outs = []
    for hh in range(y.shape[1] // HEAD_DIM):
        t = y[:, hh * HEAD_DIM:(hh + 1) * HEAD_DIM]
        ms = jnp.mean(t * t, axis=-1, keepdims=True)
        t = t * lax.rsqrt(ms + EPS) * g
        if cos is not None:
            partner = jnp.where(in_x1, pltpu.roll(t, HEAD_DIM - AXIS_DIM // 2, 1), pltpu.roll(t, AXIS_DIM // 2, 1))
            t = t * cos + partner * sin
        outs.append(t)
    return jnp.concatenate(outs, axis=1)


def _inproj_kernel(*refs, tn, nq, rows, rope, want_q, want_aux):
    it = iter(refs)
    h_ref, shift_ref, scale_ref, g1_ref, wqk_ref, wr_ref, qg_ref, kg_ref = (next(it) for _ in range(8))
    cos_ref = sin_ref = None
    if rope:
        cos_ref, sin_ref = next(it), next(it)
    q_ref = next(it) if want_q else None
    k_ref, v_ref = next(it), next(it)
    aux_ref = next(it) if want_aux else None
    xn_ref = next(it)

    gs = g1_ref[...] * (1.0 + scale_ref[...])
    shift = shift_ref[...]
    qg = qg_ref[...] * (ATT_SCALE * LOG2E)
    kg = kg_ref[...]

    for r, nrows in _chunk_starts(rows):
        rs = slice(r, r + nrows)
        _norm_to(xn_ref, h_ref.at[rs, :], gs, shift, row0=r)
        cos = cos_ref[rs, :] if rope else None
        sin = sin_ref[rs, :] if rope else None

        def tile(w_ref, j):
            return jnp.dot(xn_ref[rs, :], w_ref[:, j * tn:(j + 1) * tn], preferred_element_type=F32)

        if want_q:
            for j in range(nq):
                q_ref[rs, j * tn:(j + 1) * tn] = _head_norm_rope(tile(wqk_ref, j), qg, cos, sin).astype(BF16)
        k_ref[rs, :] = _head_norm_rope(tile(wqk_ref, nq), kg, cos, sin).astype(BF16)
        v_ref[rs, :] = tile(wr_ref, 0).astype(BF16)
        if want_aux:
            for j in range(aux_ref.shape[1] // tn):
                aux_ref[rs, j * tn:(j + 1) * tn] = tile(wr_ref, 1 + j)


def _inproj(h, mods3, row0, g1, w_in, q_g, k_g, rope_tabs, *, tm, want_q=True, want_aux=True):
    bx, lx, d = h.shape
    tn = KV_WIDTH
    nq = ATT_WIDTH // tn
    rope = rope_tabs is not None
    nqk = ATT_WIDTH + KV_WIDTH
    n_rest = w_in.shape[1] - nqk
    assert nqk % n_rest == 0
    aux_w = n_rest - KV_WIDTH
    row = lambda b, i: (b, i, 0)

    in_specs = [
        pl.BlockSpec((None, tm, d), row),
        pl.BlockSpec((None, 1, d), _mod_spec(row0, 0)),
        pl.BlockSpec((None, 1, d), _mod_spec(row0, 1)),
        _const_spec((1, d)),
        pl.BlockSpec((d, nqk), lambda b, i: (0, 0), pipeline_mode=pl.Buffered(1)),
        pl.BlockSpec((d, n_rest), lambda b, i: (0, nqk // n_rest), pipeline_mode=pl.Buffered(1)),
        _const_spec((1, HEAD_DIM)),
        _const_spec((1, HEAD_DIM)),
    ]
    args = [h, mods3, mods3, g1.reshape(1, d), w_in, w_in, q_g.reshape(1, HEAD_DIM), k_g.reshape(1, HEAD_DIM)]
    if rope:
        in_specs += [pl.BlockSpec((tm, HEAD_DIM), lambda b, i: (i, 0))] * 2
        args += list(rope_tabs)

    out_specs, out_shape = [], []
    if want_q:
        out_specs.append(pl.BlockSpec((None, tm, ATT_WIDTH), row))
        out_shape.append(jax.ShapeDtypeStruct((bx, lx, ATT_WIDTH), BF16))
    for _ in range(2):
        out_specs.append(pl.BlockSpec((None, tm, KV_WIDTH), row))
        out_shape.append(jax.ShapeDtypeStruct((bx, lx, KV_WIDTH), BF16))
    if want_aux:
        out_specs.append(pl.BlockSpec((None, tm, aux_w), row))
        out_shape.append(jax.ShapeDtypeStruct((bx, lx, aux_w), F32))

    return pl.pallas_call(
        functools.partial(_inproj_kernel, tn=tn, nq=nq, rows=_row_chunks(tm, IN_CHUNKS), rope=rope, want_q=want_q,
                          want_aux=want_aux),
        grid=(bx, lx // tm),
        in_specs=in_specs,
        out_specs=out_specs,
        out_shape=out_shape,
        scratch_shapes=[pltpu.VMEM((tm, d), BF16)],
        name="inproj",
        compiler_params=_params(("arbitrary", "arbitrary")),
    )(*args)


def _transpose_bf16(x):
    eye = (lax.broadcasted_iota(jnp.int32, (HEAD_DIM, HEAD_DIM), 0)
           == lax.broadcasted_iota(jnp.int32, (HEAD_DIM, HEAD_DIM), 1)).astype(BF16)
    return lax.dot_general(eye, x, (((1,), (1,)), ((), ())), preferred_element_type=F32).astype(BF16)


def _scores(kch, q, mask):
    s = lax.dot_general(kch, q, (((1,), (1,)), ((), ())), preferred_element_type=F32)
    return s if mask is None else jnp.where(mask, s, NEG_INF)


def _softmax_step(state, s, vt):
    m_prev, acc = state
    m_next = jnp.maximum(m_prev, jnp.max(s, axis=0, keepdims=True))
    p = jnp.exp2(s - m_next).astype(BF16)
    alpha = jnp.exp2(m_prev - m_next)
    acc = alpha * acc + jnp.dot(vt, p, preferred_element_type=F32)
    return m_next, acc


def _attn_kernel(*refs, mode, tq, nsub, tk, seq, has_sink):
    it = iter(refs)
    q_ref = next(it)
    k_ref = v_ref = None
    if mode != "ctx":
        k_ref, v_ref = next(it), next(it)
    kc_ref, vc_ref = next(it), next(it)
    sink_ref = next(it) if has_sink else None
    o_ref = next(it)
    vt_ref = next(it) if mode != "ctx" else None
    vct_ref = next(it)

    kvh = pl.program_id(1)
    step = pl.program_id(2)

    @pl.when(step == 0)
    def _():
        vct_ref[:HEAD_DIM, :] = _transpose_bf16(vc_ref[...])
        vct_ref[HEAD_DIM:, :] = jnp.ones((VT_ROWS - HEAD_DIM, vct_ref.shape[1]), BF16)
        if mode != "ctx":
            vt_ref[HEAD_DIM:, :] = jnp.ones((VT_ROWS - HEAD_DIM, seq), BF16)
            for c in range(0, seq, tk):
                vt_ref[:HEAD_DIM, c:c + tk] = _transpose_bf16(v_ref[c:c + tk, :])

    n = GROUPS * tq
    if mode == "window":
        band0 = (lax.broadcasted_iota(jnp.int32, (tk, n), 0)
                 - (lax.broadcasted_iota(jnp.int32, (tk, n), 1) & (tq - 1)) + WINDOW)
        interior_mask = (band0 - WINDOW).astype(jnp.uint32) <= 2 * WINDOW

    items = []
    for sub in range(nsub):
        start = (step * nsub + sub) * tq
        rows = slice(sub * tq, (sub + 1) * tq)
        q = jnp.concatenate([q_ref[rows, g * HEAD_DIM:(g + 1) * HEAD_DIM] for g in range(GROUPS)], axis=0)
        if mode == "dense":
            for c in range(0, seq, tk):
                items.append((sub, q, None, (lambda c=c: k_ref[c:c + tk, :]), (lambda c=c: vt_ref[:, c:c + tk])))
        elif mode == "window":
            ks = pl.multiple_of(jnp.clip(start - WINDOW, 0, seq - tk), WINDOW)
            if 0 < sub < nsub - 1:
                mask = interior_mask
            else:
                mask = (band0 + (ks - start)).astype(jnp.uint32) <= 2 * WINDOW
            items.append((sub, q, mask, (lambda ks=ks: k_ref[pl.ds(ks, tk), :]),
                          (lambda ks=ks: vt_ref[:, pl.ds(ks, tk)])))
        items.append((sub, q, None, (lambda: kc_ref[...]), (lambda: vct_ref[...])))

    per_sub = len(items) // nsub
    items = [items[sub * per_sub + c] for c in range(per_sub) for sub in range(nsub)]

    states = [(jnp.full((1, n), NEG_INF, F32), jnp.zeros((VT_ROWS, n), F32)) for _ in range(nsub)]
    score = lambda item: _scores(item[3](), item[1], item[2])
    pending = [score(item) for item in items[:SCORE_LOOKAHEAD]]
    for idx, (sub, _, _, _, vt_fn) in enumerate(items):
        s_cur = pending.pop(0)
        if idx + SCORE_LOOKAHEAD < len(items):
            pending.append(score(items[idx + SCORE_LOOKAHEAD]))
        states[sub] = _softmax_step(states[sub], s_cur, vt_fn())

    for sub, (m, acc) in enumerate(states):
        l = acc[HEAD_DIM:HEAD_DIM + 1, :]
        acc = acc[:HEAD_DIM, :]
        if has_sink:
            sink = jnp.concatenate([jnp.full((1, tq), sink_ref[kvh * GROUPS + g] * LOG2E, F32)
                                    for g in range(GROUPS)], axis=1)
            l = l + jnp.exp2(sink - m)
        out = acc / l
        for g in range(GROUPS):
            o_ref[sub * tq:(sub + 1) * tq, g * HEAD_DIM:(g + 1) * HEAD_DIM] = (
                out[:, g * tq:(g + 1) * tq].T.astype(BF16))


def _attention(q, k, v, kc, vc, sink, *, mode, tq, nsub, tk):
    b, lq, _ = q.shape
    lc = kc.shape[1]
    seq = k.shape[1] if k is not None else 0
    gw = GROUPS * HEAD_DIM
    tstep = tq * nsub
    in_specs = [pl.BlockSpec((None, tstep, gw), lambda bb, h, i: (bb, i, h))]
    args = [q]
    scratch = []
    if mode != "ctx":
        in_specs += [pl.BlockSpec((None, seq, HEAD_DIM), lambda bb, h, i: (bb, 0, h))] * 2
        args += [k, v]
        scratch.append(pltpu.VMEM((VT_ROWS, seq), BF16))
    in_specs += [pl.BlockSpec((None, lc, HEAD_DIM), lambda bb, h, i: (bb, 0, h))] * 2
    args += [kc, vc]
    scratch.append(pltpu.VMEM((VT_ROWS, lc), BF16))
    has_sink = sink is not None
    if has_sink:
        in_specs.append(pl.BlockSpec(memory_space=pltpu.SMEM))
        args.append(sink)
    return pl.pallas_call(
        functools.partial(_attn_kernel, mode=mode, tq=tq, nsub=nsub, tk=tk, seq=seq, has_sink=has_sink),
        grid=(b, KV_HEADS, lq // tstep),
        in_specs=in_specs,
        out_specs=pl.BlockSpec((None, tstep, gw), lambda bb, h, i: (bb, i, h)),
        out_shape=jax.ShapeDtypeStruct((b, lq, ATT_WIDTH), BF16),
        scratch_shapes=scratch,
        name="attn_" + mode,
        compiler_params=_params(("arbitrary", "arbitrary", "arbitrary")),
    )(*args)


def _conv_mixer(g, aux_ref, prev_ref, next_ref, w_ref, row0, seq_len):
    tm, w = aux_ref.shape[0], aux_ref.shape[1] // 3
    gb, gc, uu = (slice(k * w + g * LANES, k * w + (g + 1) * LANES) for k in range(3))
    z = aux_ref[:, gc] * aux_ref[:, uu]
    z_prev = prev_ref[HALO - 1:HALO, gc] * prev_ref[HALO - 1:HALO, uu]
    z_next = next_ref[0:1, gc] * next_ref[0:1, uu]
    row = lax.broadcasted_iota(jnp.int32, (tm, LANES), 0)
    pos = (row0 + row) & (seq_len - 1)
    down = jnp.where(row == 0, z_prev, pltpu.roll(z, 1, 0))
    down = jnp.where(pos == 0, 0.0, down)
    up = jnp.where(row == tm - 1, z_next, pltpu.roll(z, tm - 1, 0))
    up = jnp.where(pos == seq_len - 1, 0.0, up)
    cw = w_ref[:, g * LANES:(g + 1) * LANES]
    return aux_ref[:, gb] * (cw[0:1, :] * down + cw[1:2, :] * z + cw[2:3, :] * up)


def _pool_mixer(g, u_ref, prev_ref, next_ref, w_ref, s_ref, row0, seq_len):
    tm = u_ref.shape[0]
    at_start = (row0 & (seq_len - 1)) == 0
    at_end = ((row0 + tm) & (seq_len - 1)) == 0
    pos = (row0 & (seq_len - 1)) + lax.broadcasted_iota(jnp.int32, (tm, LANES), 0)
    cols = slice(g * LANES, (g + 1) * LANES)
    u = u_ref[:, cols]
    ext = jnp.concatenate([jnp.where(at_start, 0.0, prev_ref[:, cols]), u,
                           jnp.where(at_end, 0.0, next_ref[:, cols])], axis=0)
    n = ext.shape[0]
    half = POOL_WINDOWS[g] // 2
    fwd = ext
    bwd = pltpu.roll(ext, 1, 0)
    k = 1
    while k < half:
        fwd = fwd + pltpu.roll(fwd, n - k, 0)
        bwd = bwd + pltpu.roll(bwd, k, 0)
        k *= 2
    cnt = jnp.minimum(pos + half, seq_len) - jnp.maximum(pos - half, 0)
    pooled = (fwd + bwd)[HALO:HALO + tm, :] / cnt.astype(F32) - u
    mixed = jnp.dot(pooled.astype(BF16), w_ref[g], preferred_element_type=F32)
    return mixed * s_ref[:, cols]


def _outproj_kernel(*refs, mixer, rows, seq_len):
    it = iter(refs)
    a_ref, aux_ref, prev_ref, next_ref = (next(it) for _ in range(4))
    mix_refs = [next(it) for _ in range(1 if mixer == "conv" else 2)]
    wa_ref, wx_ref, h_ref, gate_ref, shift_ref, scale_ref, g2_ref, o_ref, hn_ref, x_ref = it

    row0 = pl.program_id(1) * a_ref.shape[0]
    mix = _conv_mixer if mixer == "conv" else _pool_mixer
    gate = gate_ref[...]
    gs = g2_ref[...] * (1.0 + scale_ref[...])
    shift = shift_ref[...]
    n_groups = x_ref.shape[1] // LANES
    ncols = o_ref.shape[1] // n_groups
    for r, nrows in _chunk_starts(rows):
        rs = slice(r, r + nrows)
        for g in range(n_groups):
            cs = slice(g * ncols, (g + 1) * ncols)
            o_ref[rs, cs] = h_ref[rs, cs] + gate[:, cs] * jnp.dot(a_ref[rs, :], wa_ref[:, cs],
                                                               preferred_element_type=F32)
            if r == 0:
                x_ref[:, g * LANES:(g + 1) * LANES] = mix(g, aux_ref, prev_ref, next_ref, *mix_refs,
                                                          row0, seq_len).astype(BF16)
        o_ref[rs, :] += gate * jnp.dot(x_ref[rs, :], wx_ref[...], preferred_element_type=F32)
        _norm_to(hn_ref, o_ref.at[rs, :], gs, shift, row0=r)


def _outproj(attn, aux, mixer, mix_params, seq_len, w_out, h, mods3, row0, g2, *, tm):
    bx, lx, d = h.shape
    wa, aux_w = attn.shape[2], aux.shape[2]
    wx = d - wa
    row = lambda b, i: (b, i, 0)
    halo_blocks = tm // HALO
    mix_specs = [_const_spec(p.shape) for p in mix_params]
    return pl.pallas_call(
        functools.partial(_outproj_kernel, mixer=mixer, rows=_row_chunks(tm, OUT_CHUNKS), seq_len=seq_len),
        grid=(bx, lx // tm),
        in_specs=[pl.BlockSpec((None, tm, wa), row),
                  pl.BlockSpec((None, tm, aux_w), row),
                  pl.BlockSpec((None, HALO, aux_w), lambda b, i: (b, jnp.maximum(i * halo_blocks - 1, 0), 0)),
                  pl.BlockSpec((None, HALO, aux_w),
                               lambda b, i: (b, jnp.minimum((i + 1) * halo_blocks, lx // HALO - 1), 0)),
                  *mix_specs,
                  pl.BlockSpec((wa, d), lambda b, i: (0, 0), pipeline_mode=pl.Buffered(1)),
                  pl.BlockSpec((wx, d), lambda b, i: (wa // wx, 0), pipeline_mode=pl.Buffered(1)),
                  pl.BlockSpec((None, tm, d), row),
                  pl.BlockSpec((None, 1, d), _mod_spec(row0, 2)),
                  pl.BlockSpec((None, 1, d), _mod_spec(row0, 3)),
                  pl.BlockSpec((None, 1, d), _mod_spec(row0, 4)),
                  _const_spec((1, d))],
        out_specs=[pl.BlockSpec((None, tm, d), row), pl.BlockSpec((None, tm, d), row)],
        out_shape=[jax.ShapeDtypeStruct((bx, lx, d), F32), jax.ShapeDtypeStruct((bx, lx, d), BF16)],
        scratch_shapes=[pltpu.VMEM((tm, wx), BF16)],
        name="outproj_" + mixer,
        compiler_params=_params(("arbitrary", "arbitrary")),
    )(attn, aux, aux, aux, *mix_params, w_out, w_out, h, mods3, mods3, mods3, g2.reshape(1, d))


def _ffn_kernel(*refs, final, th, rows, ncols):
    it = iter(refs)
    hn_ref, h_hbm, gate_ref, wg_hbm, wu_hbm, wd_hbm = (next(it) for _ in range(6))
    fg_ref = next(it) if final else None
    o_ref, wg_buf, wu_buf, wd_buf, h_buf, w_sem, h_sem = (next(it) for _ in range(7))

    b, i = pl.program_id(0), pl.program_id(1)
    tm, d = o_ref.shape
    n_hid = wg_hbm.shape[1] // th
    tile = b * pl.num_programs(1) + i
    n_steps = pl.num_programs(0) * pl.num_programs(1) * n_hid
    step0 = tile * n_hid

    def w_copies(j, slot):
        col = j * th if isinstance(j, int) else pl.multiple_of(j * th, th)
        return (pltpu.make_async_copy(wg_hbm.at[:, pl.ds(col, th)], wg_buf.at[slot], w_sem.at[0, slot]),
                pltpu.make_async_copy(wu_hbm.at[:, pl.ds(col, th)], wu_buf.at[slot], w_sem.at[1, slot]),
                pltpu.make_async_copy(wd_hbm.at[pl.ds(col, th), :], wd_buf.at[slot], w_sem.at[2, slot]))

    h_copy = pltpu.make_async_copy(h_hbm.at[pl.ds(pl.multiple_of(tile * tm, tm), tm), :], h_buf, h_sem)
    h_copy.start()

    @pl.when(tile == 0)
    def _():
        for cp in w_copies(0, 0):
            cp.start()

    gate = gate_ref[...]

    def hidden_step(j, first=False, last=False):
        step = step0 + j
        slot = lax.rem(step, 2)
        for cp in w_copies(j, slot):
            cp.wait()

        @pl.when(step + 1 < n_steps)
        def _():
            nxt = jnp.where(j + 1 < n_hid, j + 1, 0)
            for cp in w_copies(nxt, 1 - slot):
                cp.start()

        wg, wu, wd = wg_buf.at[slot], wu_buf.at[slot], wd_buf.at[slot]
        for r in range(0, tm, rows):
            rs = slice(r, r + rows)
            hn = hn_ref[rs, :]
            gt = jnp.dot(hn, wg[...], preferred_element_type=F32)
            up = jnp.dot(hn, wu[...], preferred_element_type=F32)
            act = (gt * jax.nn.sigmoid(gt) * up).astype(BF16)
            if last and r == 0:
                h_copy.wait()
            for c in range(0, d, ncols):
                cs = slice(c, c + ncols)
                dn = gate[:, cs] * jnp.dot(act, wd[:, cs], preferred_element_type=F32)
                if first:
                    o_ref[rs, cs] = dn
                elif last:
                    o_ref[rs, cs] += dn + h_buf[rs, cs]
                else:
                    o_ref[rs, cs] += dn

    hidden_step(0, first=True)

    def loop_body(j, carry):
        hidden_step(j)
        return carry

    lax.fori_loop(1, n_hid - 1, loop_body, 0)
    hidden_step(n_hid - 1, last=True)

    if final:
        fg = fg_ref[...]
        for r in range(0, tm, NORM_ROWS):
            x = o_ref[r:r + NORM_ROWS, :]
            ms = jnp.mean(x * x, axis=-1, keepdims=True)
            o_ref[r:r + NORM_ROWS, :] = x * lax.rsqrt(ms + EPS) * fg


def _ffn(hn, h, mods3, row0, w_gate, w_up, w_down, final_g, *, tm, th, rows, ncols):
    bx, lx, d = h.shape
    final = final_g is not None
    row = lambda b, i: (b, i, 0)
    any_spec = pl.BlockSpec(memory_space=pl.ANY)
    in_specs = [pl.BlockSpec((None, tm, d), row), any_spec,
                pl.BlockSpec((None, 1, d), _mod_spec(row0, 5)),
                any_spec, any_spec, any_spec]
    args = [hn, h.reshape(bx * lx, d), mods3, w_gate, w_up, w_down]
    if final:
        in_specs.append(_const_spec((1, d)))
        args.append(final_g.reshape(1, d))
    return pl.pallas_call(
        functools.partial(_ffn_kernel, final=final, th=th, rows=rows, ncols=ncols),
        grid=(bx, lx // tm),
        in_specs=in_specs,
        out_specs=pl.BlockSpec((tm, d), lambda b, i: (b * (lx // tm) + i, 0)),
        out_shape=jax.ShapeDtypeStruct((bx * lx, d), F32),
        scratch_shapes=[pltpu.VMEM((2, d, th), BF16), pltpu.VMEM((2, d, th), BF16), pltpu.VMEM((2, th, d), BF16),
                        pltpu.VMEM((tm, d), F32), pltpu.SemaphoreType.DMA((3, 2)), pltpu.SemaphoreType.DMA(())],
        name="ffn",
        compiler_params=_params(("arbitrary", "arbitrary")),
    )(*args).reshape(bx, lx, d)


def _rope_tables(n):
    pos = np.arange(n)
    inv = np.power(ROPE_THETA, -np.arange(0, AXIS_DIM, 2, dtype=np.float64) / AXIS_DIM)
    ang_r = (pos // GRID_W)[:, None] * inv
    ang_c = (pos % GRID_W)[:, None] * inv
    cos = np.concatenate([np.cos(ang_r), np.cos(ang_r), np.cos(ang_c), np.cos(ang_c)], axis=-1)
    sin = np.concatenate([-np.sin(ang_r), np.sin(ang_r), -np.sin(ang_c), np.sin(ang_c)], axis=-1)
    return jnp.asarray(cos, F32), jnp.asarray(sin, F32)


def kernel(x, c, ctx, c_ctx, l0_norm1_g, l0_w_mod, l0_b_mod, l0_w_in, l0_q_norm_g, l0_k_norm_g, l0_conv_w, l0_w_out, l0_norm2_g, l0_w_gate, l0_w_up, l0_w_down, l1_norm1_g, l1_w_mod, l1_b_mod, l1_w_in, l1_q_norm_g, l1_k_norm_g, l1_sink, l1_pool_w, l1_pool_scale, l1_w_out, l1_norm2_g, l1_w_gate, l1_w_up, l1_w_down, final_norm_g):
    b, s, d = x.shape
    lc = ctx.shape[1]
    rope = _rope_tables(s)
    c_all = jnp.zeros((MOD_ROWS, d), F32).at[:b].set(c).at[b].set(c_ctx)
    ctx_row = b
    bf = lambda w: w.astype(BF16)
    tm_in, tm_out = 512, 512
    ffn_tiles = dict(tm=1024, th=512, rows=512, ncols=512)

    mods = _modulation(c_all, l0_w_mod, l0_b_mod).reshape(MOD_ROWS, 1, N_MOD * d)
    w_in, w_out = bf(l0_w_in), bf(l0_w_out)
    q_g, k_g = l0_q_norm_g, l0_k_norm_g
    w_gate, w_up, w_down = bf(l0_w_gate), bf(l0_w_up), bf(l0_w_down)
    hc = ctx.reshape(1, b * lc, d)

    q, k, v, aux = _inproj(x, mods, 0, l0_norm1_g, w_in, q_g, k_g, rope, tm=tm_in)
    qc, kc, vc, auxc = _inproj(hc, mods, ctx_row, l0_norm1_g, w_in, q_g, k_g, None, tm=tm_in)
    qc, kc, vc = (t.reshape(b, lc, -1) for t in (qc, kc, vc))

    attn = _attention(q, k, v, kc, vc, None, mode="dense", tq=256, nsub=2, tk=512)
    h, hn = _outproj(attn, aux, "conv", (l0_conv_w,), s, w_out, x, mods, 0, l0_norm2_g, tm=tm_out)
    h = _ffn(hn, h, mods, 0, w_gate, w_up, w_down, None, **ffn_tiles)

    attn_c = _attention(qc, None, None, kc, vc, None, mode="ctx", tq=lc, nsub=1, tk=lc)
    hc, hcn = _outproj(attn_c.reshape(1, b * lc, -1), auxc, "conv", (l0_conv_w,), lc, w_out, hc, mods, ctx_row,
                       l0_norm2_g, tm=tm_out)
    hc = _ffn(hcn, hc, mods, ctx_row, w_gate, w_up, w_down, None, **ffn_tiles)

    mods = _modulation(c_all, l1_w_mod, l1_b_mod).reshape(MOD_ROWS, 1, N_MOD * d)
    w_in, w_out = bf(l1_w_in), bf(l1_w_out)
    q_g, k_g = l1_q_norm_g, l1_k_norm_g
    w_gate, w_up, w_down = bf(l1_w_gate), bf(l1_w_up), bf(l1_w_down)

    q, k, v, u = _inproj(h, mods, 0, l1_norm1_g, w_in, q_g, k_g, rope, tm=tm_in)
    kc, vc = _inproj(hc, mods, ctx_row, l1_norm1_g, w_in, q_g, k_g, None, tm=tm_in,
                     want_q=False, want_aux=False)
    kc, vc = kc.reshape(b, lc, -1), vc.reshape(b, lc, -1)

    attn = _attention(q, k, v, kc, vc, l1_sink, mode="window", tq=256, nsub=8, tk=512)
    pool_params = (bf(l1_pool_w), l1_pool_scale.reshape(1, -1))
    h, hn = _outproj(attn, u, "pool", pool_params, s, w_out, h, mods, 0, l1_norm2_g, tm=tm_out)
    return _ffn(hn, h, mods, 0, w_gate, w_up, w_down, final_norm_g, **ffn_tiles)
```

```python
import functools
import math

import jax
import jax.numpy as jnp
import numpy as np
from jax import lax
from jax.experimental import pallas as pl
from jax.experimental.pallas import tpu as pltpu

F32 = jnp.float32
BF16 = jnp.bfloat16

LANES = 128
HEAD_DIM = 128
AXIS_DIM = HEAD_DIM // 2
Q_HEADS = 12
KV_HEADS = 4
GROUPS = Q_HEADS // KV_HEADS
ATT_WIDTH = Q_HEADS * HEAD_DIM
KV_WIDTH = KV_HEADS * HEAD_DIM
GRID_W = 64
WINDOW = 128
POOL_WINDOWS = (2, 4, 8, 16)
ROPE_THETA = 10000.0
N_MOD = 6
EPS = 1e-6
NEG_INF = -1e30
ATT_SCALE = HEAD_DIM ** -0.5
LOG2E = math.log2(math.e)
MOD_ROWS = 8
VMEM_LIMIT = 56 * 1024 * 1024
NORM_ROWS = 16
CAST_ROWS, CAST_COLS = 512, 2048
HALO = 8
SCORE_LOOKAHEAD = 2
VT_ROWS = HEAD_DIM + 16


def _params(sem):
    return pltpu.CompilerParams(dimension_semantics=sem, vmem_limit_bytes=VMEM_LIMIT)


def _mod_spec(row0, chunk):
    def imap(b, *_):
        return (b + row0, 0, chunk)
    return imap


def _const_spec(shape):
    zeros = (0,) * len(shape)
    return pl.BlockSpec(shape, lambda *_: zeros, pipeline_mode=pl.Buffered(1))


def _mod_kernel(c_ref, w_ref, b_ref, o_ref):
    c = c_ref[...]
    sc = c * jax.nn.sigmoid(c)
    o_ref[...] = jnp.dot(sc.astype(BF16), w_ref[...].astype(BF16),
                         preferred_element_type=F32) + b_ref[...]


def _modulation(c_all, w_mod, b_mod):
    d, n = w_mod.shape
    tn = 1024
    return pl.pallas_call(
        _mod_kernel,
        grid=(n // tn,),
        in_specs=[pl.BlockSpec((MOD_ROWS, d), lambda j: (0, 0)),
                  pl.BlockSpec((d, tn), lambda j: (0, j)),
                  pl.BlockSpec((1, tn), lambda j: (0, j))],
        out_specs=pl.BlockSpec((MOD_ROWS, tn), lambda j: (0, j)),
        out_shape=jax.ShapeDtypeStruct((MOD_ROWS, n), F32),
        name="modulation",
        compiler_params=_params(("arbitrary",)),
    )(c_all, w_mod, b_mod.reshape(1, n))


def _rms_modulate_rows(x, gs, shift):
    ms = jnp.mean(x * x, axis=-1, keepdims=True)
    return x * lax.rsqrt(ms + EPS) * gs + shift


def _norm_to(dst_ref, src, gs, shift, row0=0):
    rows = src.shape[0]
    for r in range(0, rows, NORM_ROWS):
        x = src[r:r + NORM_ROWS, :]
        dst_ref[row0 + r:row0 + r + NORM_ROWS, :] = _rms_modulate_rows(x, gs, shift).astype(BF16)


def _head_norm_rope(y, g, cos, sin):
    lane = lax.broadcasted_iota(jnp.int32, (1, HEAD_DIM), 1)
    in_x1 = (lane & (AXIS_DIM // 2)) == 0
    outs = []
    for hh in range(y.shape[1] // HEAD_DIM):
        t = y[:, hh * HEAD_DIM:(hh + 1) * HEAD_DIM]
        ms = jnp.mean(t * t, axis=-1, keepdims=True)
        t = t * lax.rsqrt(ms + EPS) * g
        if cos is not None:
            partner = jnp.where(in_x1, pltpu.roll(t, HEAD_DIM - AXIS_DIM // 2, 1), pltpu.roll(t, AXIS_DIM // 2, 1))
            t = t * cos + partner * sin
        outs.append(t)
    return jnp.concatenate(outs, axis=1)


def _inproj_kernel(*refs, tn, nq, rows, rope, want_q, want_aux):
    it = iter(refs)
    h_ref, shift_ref, scale_ref, g1_ref, wqk_ref, wr_ref, qg_ref, kg_ref = (next(it) for _ in range(8))
    cos_ref = sin_ref = None
    if rope:
        cos_ref, sin_ref = next(it), next(it)
    q_ref = next(it) if want_q else None
    k_ref, v_ref = next(it), next(it)
    aux_ref = next(it) if want_aux else None
    xn_ref = next(it)

    gs = g1_ref[...] * (1.0 + scale_ref[...])
    shift = shift_ref[...]
    qg = qg_ref[...] * (ATT_SCALE * LOG2E)
    kg = kg_ref[...]

    for r in range(0, h_ref.shape[0], rows):
        rs = slice(r, r + rows)
        _norm_to(xn_ref, h_ref.at[rs, :], gs, shift, row0=r)
        cos = cos_ref[rs, :] if rope else None
        sin = sin_ref[rs, :] if rope else None

        def tile(w_ref, j):
            return jnp.dot(xn_ref[rs, :], w_ref[:, j * tn:(j + 1) * tn], preferred_element_type=F32)

        if want_q:
            for j in range(nq):
                q_ref[rs, j * tn:(j + 1) * tn] = _head_norm_rope(tile(wqk_ref, j), qg, cos, sin).astype(BF16)
        k_ref[rs, :] = _head_norm_rope(tile(wqk_ref, nq), kg, cos, sin).astype(BF16)
        v_ref[rs, :] = tile(wr_ref, 0).astype(BF16)
        if want_aux:
            for j in range(aux_ref.shape[1] // tn):
                aux_ref[rs, j * tn:(j + 1) * tn] = tile(wr_ref, 1 + j)


def _inproj(h, mods3, row0, g1, w_in, q_g, k_g, rope_tabs, *, tm, want_q=True, want_aux=True):
    bx, lx, d = h.shape
    tn = KV_WIDTH
    nq = ATT_WIDTH // tn
    rope = rope_tabs is not None
    nqk = ATT_WIDTH + KV_WIDTH
    n_rest = w_in.shape[1] - nqk
    assert nqk % n_rest == 0
    aux_w = n_rest - KV_WIDTH
    row = lambda b, i: (b, i, 0)

    in_specs = [
        pl.BlockSpec((None, tm, d), row),
        pl.BlockSpec((None, 1, d), _mod_spec(row0, 0)),
        pl.BlockSpec((None, 1, d), _mod_spec(row0, 1)),
        _const_spec((1, d)),
        pl.BlockSpec((d, nqk), lambda b, i: (0, 0), pipeline_mode=pl.Buffered(1)),
        pl.BlockSpec((d, n_rest), lambda b, i: (0, nqk // n_rest), pipeline_mode=pl.Buffered(1)),
        _const_spec((1, HEAD_DIM)),
        _const_spec((1, HEAD_DIM)),
    ]
    args = [h, mods3, mods3, g1.reshape(1, d), w_in, w_in, q_g.reshape(1, HEAD_DIM), k_g.reshape(1, HEAD_DIM)]
    if rope:
        in_specs += [pl.BlockSpec((tm, HEAD_DIM), lambda b, i: (i, 0))] * 2
        args += list(rope_tabs)

    out_specs, out_shape = [], []
    if want_q:
        out_specs.append(pl.BlockSpec((None, tm, ATT_WIDTH), row))
        out_shape.append(jax.ShapeDtypeStruct((bx, lx, ATT_WIDTH), BF16))
    for _ in range(2):
        out_specs.append(pl.BlockSpec((None, tm, KV_WIDTH), row))
        out_shape.append(jax.ShapeDtypeStruct((bx, lx, KV_WIDTH), BF16))
    if want_aux:
        out_specs.append(pl.BlockSpec((None, tm, aux_w), row))
        out_shape.append(jax.ShapeDtypeStruct((bx, lx, aux_w), F32))

    return pl.pallas_call(
        functools.partial(_inproj_kernel, tn=tn, nq=nq, rows=tm // 2, rope=rope, want_q=want_q,
                          want_aux=want_aux),
        grid=(bx, lx // tm),
        in_specs=in_specs,
        out_specs=out_specs,
        out_shape=out_shape,
        scratch_shapes=[pltpu.VMEM((tm, d), BF16)],
        name="inproj",
        compiler_params=_params(("arbitrary", "arbitrary")),
    )(*args)


def _transpose_bf16(x):
    eye = (lax.broadcasted_iota(jnp.int32, (HEAD_DIM, HEAD_DIM), 0)
           == lax.broadcasted_iota(jnp.int32, (HEAD_DIM, HEAD_DIM), 1)).astype(BF16)
    return lax.dot_general(eye, x, (((1,), (1,)), ((), ())), preferred_element_type=F32).astype(BF16)


def _scores(kch, q, mask):
    s = lax.dot_general(kch, q, (((1,), (1,)), ((), ())), preferred_element_type=F32)
    return s if mask is None else jnp.where(mask, s, NEG_INF)


def _softmax_step(state, s, vt):
    m_prev, acc = state
    m_next = jnp.maximum(m_prev, jnp.max(s, axis=0, keepdims=True))
    p = jnp.exp2(s - m_next).astype(BF16)
    alpha = jnp.exp2(m_prev - m_next)
    acc = alpha * acc + jnp.dot(vt, p, preferred_element_type=F32)
    return m_next, acc


def _cast_job(t, w_f32, w_bf16, buf_in, buf_out, sem_in, sem_out):
    counts = [w.shape[0] // CAST_ROWS for w in w_f32]
    first = [sum(counts[:a]) for a in range(len(counts))]
    n_jobs = sum(counts)
    slot = lax.rem(t, 2)

    def fetch(a, c, sl):
        return pltpu.make_async_copy(w_f32[a].at[pl.ds(pl.multiple_of(c * CAST_ROWS, CAST_ROWS), CAST_ROWS), :],
                                     buf_in.at[sl], sem_in.at[sl])

    def store(a, c, sl):
        return pltpu.make_async_copy(buf_out.at[sl],
                                     w_bf16[a].at[pl.ds(pl.multiple_of(c * CAST_ROWS, CAST_ROWS), CAST_ROWS), :],
                                     sem_out.at[sl])

    @pl.when(t == 0)
    def _():
        fetch(0, 0, 0).start()

    for a in range(len(counts)):
        @pl.when((t + 1 >= first[a]) & (t + 1 < first[a] + counts[a]))
        def _(a=a):
            fetch(a, t + 1 - first[a], 1 - slot).start()

    @pl.when(t < n_jobs)
    def _():
        fetch(0, 0, slot).wait()

    @pl.when((t >= 2) & (t < n_jobs + 2))
    def _():
        store(0, 0, slot).wait()

    @pl.when(t < n_jobs)
    def _():
        buf_out[slot] = buf_in[slot].astype(BF16)

    for a in range(len(counts)):
        @pl.when((t >= first[a]) & (t < first[a] + counts[a]))
        def _(a=a):
            store(a, t - first[a], slot).start()


def _attn_kernel(*refs, mode, tq, nsub, tk, seq, has_sink, n_cast):
    it = iter(refs)
    q_ref = next(it)
    k_ref = v_ref = None
    if mode != "ctx":
        k_ref, v_ref = next(it), next(it)
    kc_ref, vc_ref = next(it), next(it)
    sink_ref = next(it) if has_sink else None
    cast_in = [next(it) for _ in range(n_cast)]
    o_ref = next(it)
    cast_out = [next(it) for _ in range(n_cast)]
    vt_ref = next(it) if mode != "ctx" else None
    vct_ref = next(it)

    kvh = pl.program_id(1)
    step = pl.program_id(2)
    if n_cast:
        cast_scratch = [next(it) for _ in range(4)]
        t = (pl.program_id(0) * pl.num_programs(1) + kvh) * pl.num_programs(2) + step
        _cast_job(t, cast_in, cast_out, *cast_scratch)

    @pl.when(step == 0)
    def _():
        vct_ref[:HEAD_DIM, :] = _transpose_bf16(vc_ref[...])
        vct_ref[HEAD_DIM:, :] = jnp.ones((VT_ROWS - HEAD_DIM, vct_ref.shape[1]), BF16)
        if mode != "ctx":
            vt_ref[HEAD_DIM:, :] = jnp.ones((VT_ROWS - HEAD_DIM, seq), BF16)
            for c in range(0, seq, tk):
                vt_ref[:HEAD_DIM, c:c + tk] = _transpose_bf16(v_ref[c:c + tk, :])

    n = GROUPS * tq
    if mode == "window":
        band0 = (lax.broadcasted_iota(jnp.int32, (tk, n), 0)
                 - (lax.broadcasted_iota(jnp.int32, (tk, n), 1) & (tq - 1)) + WINDOW)
        interior_mask = (band0 - WINDOW).astype(jnp.uint32) <= 2 * WINDOW

    items = []
    for sub in range(nsub):
        start = (step * nsub + sub) * tq
        rows = slice(sub * tq, (sub + 1) * tq)
        q = jnp.concatenate([q_ref[rows, g * HEAD_DIM:(g + 1) * HEAD_DIM] for g in range(GROUPS)], axis=0)
        if mode == "dense":
            for c in range(0, seq, tk):
                items.append((sub, q, None, (lambda c=c: k_ref[c:c + tk, :]), (lambda c=c: vt_ref[:, c:c + tk])))
        elif mode == "window":
            ks = pl.multiple_of(jnp.clip(start - WINDOW, 0, seq - tk), WINDOW)
            if 0 < sub < nsub - 1:
                mask = interior_mask
            else:
                mask = (band0 + (ks - start)).astype(jnp.uint32) <= 2 * WINDOW
            items.append((sub, q, mask, (lambda ks=ks: k_ref[pl.ds(ks, tk), :]),
                          (lambda ks=ks: vt_ref[:, pl.ds(ks, tk)])))
        items.append((sub, q, None, (lambda: kc_ref[...]), (lambda: vct_ref[...])))

    per_sub = len(items) // nsub
    items = [items[sub * per_sub + c] for c in range(per_sub) for sub in range(nsub)]

    states = [(jnp.full((1, n), NEG_INF, F32), jnp.zeros((VT_ROWS, n), F32)) for _ in range(nsub)]
    score = lambda item: _scores(item[3](), item[1], item[2])
    pending = [score(item) for item in items[:SCORE_LOOKAHEAD]]
    for idx, (sub, _, _, _, vt_fn) in enumerate(items):
        s_cur = pending.pop(0)
        if idx + SCORE_LOOKAHEAD < len(items):
            pending.append(score(items[idx + SCORE_LOOKAHEAD]))
        states[sub] = _softmax_step(states[sub], s_cur, vt_fn())

    for sub, (m, acc) in enumerate(states):
        l = acc[HEAD_DIM:HEAD_DIM + 1, :]
        acc = acc[:HEAD_DIM, :]
        if has_sink:
            sink = jnp.concatenate([jnp.full((1, tq), sink_ref[kvh * GROUPS + g] * LOG2E, F32)
                                    for g in range(GROUPS)], axis=1)
            l = l + jnp.exp2(sink - m)
        out = acc / l
        for g in range(GROUPS):
            o_ref[sub * tq:(sub + 1) * tq, g * HEAD_DIM:(g + 1) * HEAD_DIM] = (
                out[:, g * tq:(g + 1) * tq].T.astype(BF16))


def _attention(q, k, v, kc, vc, sink, *, mode, tq, nsub, tk, cast=()):
    b, lq, _ = q.shape
    lc = kc.shape[1]
    seq = k.shape[1] if k is not None else 0
    gw = GROUPS * HEAD_DIM
    tstep = tq * nsub
    in_specs = [pl.BlockSpec((None, tstep, gw), lambda bb, h, i: (bb, i, h))]
    args = [q]
    scratch = []
    if mode != "ctx":
        in_specs += [pl.BlockSpec((None, seq, HEAD_DIM), lambda bb, h, i: (bb, 0, h))] * 2
        args += [k, v]
        scratch.append(pltpu.VMEM((VT_ROWS, seq), BF16))
    in_specs += [pl.BlockSpec((None, lc, HEAD_DIM), lambda bb, h, i: (bb, 0, h))] * 2
    args += [kc, vc]
    scratch.append(pltpu.VMEM((VT_ROWS, lc), BF16))
    has_sink = sink is not None
    if has_sink:
        in_specs.append(pl.BlockSpec(memory_space=pltpu.SMEM))
        args.append(sink)
    out_specs = [pl.BlockSpec((None, tstep, gw), lambda bb, h, i: (bb, i, h))]
    out_shape = [jax.ShapeDtypeStruct((b, lq, ATT_WIDTH), BF16)]
    if cast:
        n_steps = b * KV_HEADS * (lq // tstep)
        assert sum(w.shape[0] // CAST_ROWS for w in cast) + 2 <= n_steps
        in_specs += [pl.BlockSpec(memory_space=pl.ANY)] * len(cast)
        args += list(cast)
        out_specs += [pl.BlockSpec(memory_space=pl.ANY)] * len(cast)
        out_shape += [jax.ShapeDtypeStruct(w.shape, BF16) for w in cast]
        scratch += [pltpu.VMEM((2, CAST_ROWS, CAST_COLS), F32), pltpu.VMEM((2, CAST_ROWS, CAST_COLS), BF16),
                    pltpu.SemaphoreType.DMA((2,)), pltpu.SemaphoreType.DMA((2,))]
    outs = pl.pallas_call(
        functools.partial(_attn_kernel, mode=mode, tq=tq, nsub=nsub, tk=tk, seq=seq, has_sink=has_sink,
                          n_cast=len(cast)),
        grid=(b, KV_HEADS, lq // tstep),
        in_specs=in_specs,
        out_specs=out_specs,
        out_shape=out_shape,
        scratch_shapes=scratch,
        name="attn_" + mode,
        compiler_params=_params(("arbitrary", "arbitrary", "arbitrary")),
    )(*args)
    return (outs[0], outs[1:]) if cast else outs[0]


def _conv_mixer(g, aux_ref, prev_ref, next_ref, w_ref, row0, seq_len):
    tm, w = aux_ref.shape[0], aux_ref.shape[1] // 3
    gb, gc, uu = (slice(k * w + g * LANES, k * w + (g + 1) * LANES) for k in range(3))
    z = aux_ref[:, gc] * aux_ref[:, uu]
    z_prev = prev_ref[HALO - 1:HALO, gc] * prev_ref[HALO - 1:HALO, uu]
    z_next = next_ref[0:1, gc] * next_ref[0:1, uu]
    row = lax.broadcasted_iota(jnp.int32, (tm, LANES), 0)
    pos = (row0 + row) & (seq_len - 1)
    down = jnp.where(row == 0, z_prev, pltpu.roll(z, 1, 0))
    down = jnp.where(pos == 0, 0.0, down)
    up = jnp.where(row == tm - 1, z_next, pltpu.roll(z, tm - 1, 0))
    up = jnp.where(pos == seq_len - 1, 0.0, up)
    cw = w_ref[:, g * LANES:(g + 1) * LANES]
    return aux_ref[:, gb] * (cw[0:1, :] * down + cw[1:2, :] * z + cw[2:3, :] * up)


def _pool_mixer(g, u_ref, prev_ref, next_ref, w_ref, s_ref, row0, seq_len):
    tm = u_ref.shape[0]
    at_start = (row0 & (seq_len - 1)) == 0
    at_end = ((row0 + tm) & (seq_len - 1)) == 0
    pos = (row0 & (seq_len - 1)) + lax.broadcasted_iota(jnp.int32, (tm, LANES), 0)
    cols = slice(g * LANES, (g + 1) * LANES)
    u = u_ref[:, cols]
    ext = jnp.concatenate([jnp.where(at_start, 0.0, prev_ref[:, cols]), u,
                           jnp.where(at_end, 0.0, next_ref[:, cols])], axis=0)
    n = ext.shape[0]
    half = POOL_WINDOWS[g] // 2
    fwd = ext
    bwd = pltpu.roll(ext, 1, 0)
    k = 1
    while k < half:
        fwd = fwd + pltpu.roll(fwd, n - k, 0)
        bwd = bwd + pltpu.roll(bwd, k, 0)
        k *= 2
    cnt = jnp.minimum(pos + half, seq_len) - jnp.maximum(pos - half, 0)
    pooled = (fwd + bwd)[HALO:HALO + tm, :] / cnt.astype(F32) - u
    mixed = jnp.dot(pooled.astype(BF16), w_ref[g], preferred_element_type=F32)
    return mixed * s_ref[:, cols]


def _outproj_kernel(*refs, mixer, rows, seq_len):
    it = iter(refs)
    a_ref, aux_ref, prev_ref, next_ref = (next(it) for _ in range(4))
    mix_refs = [next(it) for _ in range(1 if mixer == "conv" else 2)]
    wa_ref, wx_ref, h_ref, gate_ref, shift_ref, scale_ref, g2_ref, o_ref, hn_ref, x_ref = it

    row0 = pl.program_id(1) * a_ref.shape[0]
    mix = _conv_mixer if mixer == "conv" else _pool_mixer
    gate = gate_ref[...]
    gs = g2_ref[...] * (1.0 + scale_ref[...])
    shift = shift_ref[...]
    n_groups = x_ref.shape[1] // LANES
    ncols = o_ref.shape[1] // n_groups
    for r in range(0, a_ref.shape[0], rows):
        rs = slice(r, r + rows)
        for g in range(n_groups):
            cs = slice(g * ncols, (g + 1) * ncols)
            o_ref[rs, cs] = h_ref[rs, cs] + gate[:, cs] * jnp.dot(a_ref[rs, :], wa_ref[:, cs],
                                                               preferred_element_type=F32)
            if r == 0:
                x_ref[:, g * LANES:(g + 1) * LANES] = mix(g, aux_ref, prev_ref, next_ref, *mix_refs,
                                                          row0, seq_len).astype(BF16)
        o_ref[rs, :] += gate * jnp.dot(x_ref[rs, :], wx_ref[...], preferred_element_type=F32)
        _norm_to(hn_ref, o_ref.at[rs, :], gs, shift, row0=r)


def _outproj(attn, aux, mixer, mix_params, seq_len, w_out, h, mods3, row0, g2, *, tm):
    bx, lx, d = h.shape
    wa, aux_w = attn.shape[2], aux.shape[2]
    wx = d - wa
    row = lambda b, i: (b, i, 0)
    halo_blocks = tm // HALO
    mix_specs = [_const_spec(p.shape) for p in mix_params]
    return pl.pallas_call(
        functools.partial(_outproj_kernel, mixer=mixer, rows=tm // 2, seq_len=seq_len),
        grid=(bx, lx // tm),
        in_specs=[pl.BlockSpec((None, tm, wa), row),
                  pl.BlockSpec((None, tm, aux_w), row),
                  pl.BlockSpec((None, HALO, aux_w), lambda b, i: (b, jnp.maximum(i * halo_blocks - 1, 0), 0)),
                  pl.BlockSpec((None, HALO, aux_w),
                               lambda b, i: (b, jnp.minimum((i + 1) * halo_blocks, lx // HALO - 1), 0)),
                  *mix_specs,
                  pl.BlockSpec((wa, d), lambda b, i: (0, 0), pipeline_mode=pl.Buffered(1)),
                  pl.BlockSpec((wx, d), lambda b, i: (wa // wx, 0), pipeline_mode=pl.Buffered(1)),
                  pl.BlockSpec((None, tm, d), row),
                  pl.BlockSpec((None, 1, d), _mod_spec(row0, 2)),
                  pl.BlockSpec((None, 1, d), _mod_spec(row0, 3)),
                  pl.BlockSpec((None, 1, d), _mod_spec(row0, 4)),
                  _const_spec((1, d))],
        out_specs=[pl.BlockSpec((None, tm, d), row), pl.BlockSpec((None, tm, d), row)],
        out_shape=[jax.ShapeDtypeStruct((bx, lx, d), F32), jax.ShapeDtypeStruct((bx, lx, d), BF16)],
        scratch_shapes=[pltpu.VMEM((tm, wx), BF16)],
        name="outproj_" + mixer,
        compiler_params=_params(("arbitrary", "arbitrary")),
    )(attn, aux, aux, aux, *mix_params, w_out, w_out, h, mods3, mods3, mods3, g2.reshape(1, d))


def _ffn_kernel(*refs, final, th, rows, ncols):
    it = iter(refs)
    hn_ref, h_hbm, gate_ref, wg_hbm, wu_hbm, wd_hbm = (next(it) for _ in range(6))
    fg_ref = next(it) if final else None
    o_ref, wg_buf, wu_buf, wd_buf, h_buf, w_sem, h_sem = (next(it) for _ in range(7))

    b, i = pl.program_id(0), pl.program_id(1)
    tm, d = o_ref.shape
    n_hid = wg_hbm.shape[1] // th
    tile = b * pl.num_programs(1) + i
    n_steps = pl.num_programs(0) * pl.num_programs(1) * n_hid
    step0 = tile * n_hid

    def w_copies(j, slot):
        col = j * th if isinstance(j, int) else pl.multiple_of(j * th, th)
        return (pltpu.make_async_copy(wg_hbm.at[:, pl.ds(col, th)], wg_buf.at[slot], w_sem.at[0, slot]),
                pltpu.make_async_copy(wu_hbm.at[:, pl.ds(col, th)], wu_buf.at[slot], w_sem.at[1, slot]),
                pltpu.make_async_copy(wd_hbm.at[pl.ds(col, th), :], wd_buf.at[slot], w_sem.at[2, slot]))

    h_copy = pltpu.make_async_copy(h_hbm.at[pl.ds(pl.multiple_of(tile * tm, tm), tm), :], h_buf, h_sem)
    h_copy.start()

    @pl.when(tile == 0)
    def _():
        for cp in w_copies(0, 0):
            cp.start()

    gate = gate_ref[...]

    def hidden_step(j, first=False, last=False):
        step = step0 + j
        slot = lax.rem(step, 2)
        for cp in w_copies(j, slot):
            cp.wait()

        @pl.when(step + 1 < n_steps)
        def _():
            nxt = jnp.where(j + 1 < n_hid, j + 1, 0)
            for cp in w_copies(nxt, 1 - slot):
                cp.start()

        wg, wu, wd = wg_buf.at[slot], wu_buf.at[slot], wd_buf.at[slot]
        for r in range(0, tm, rows):
            rs = slice(r, r + rows)
            hn = hn_ref[rs, :]
            gt = jnp.dot(hn, wg[...], preferred_element_type=F32)
            up = jnp.dot(hn, wu[...], preferred_element_type=F32)
            act = (gt * jax.nn.sigmoid(gt) * up).astype(BF16)
            if last and r == 0:
                h_copy.wait()
            for c in range(0, d, ncols):
                cs = slice(c, c + ncols)
                dn = gate[:, cs] * jnp.dot(act, wd[:, cs], preferred_element_type=F32)
                if first:
                    o_ref[rs, cs] = dn
                elif last:
                    o_ref[rs, cs] += dn + h_buf[rs, cs]
                else:
                    o_ref[rs, cs] += dn

    hidden_step(0, first=True)

    def loop_body(j, carry):
        hidden_step(j)
        return carry

    lax.fori_loop(1, n_hid - 1, loop_body, 0)
    hidden_step(n_hid - 1, last=True)

    if final:
        fg = fg_ref[...]
        for r in range(0, tm, NORM_ROWS):
            x = o_ref[r:r + NORM_ROWS, :]
            ms = jnp.mean(x * x, axis=-1, keepdims=True)
            o_ref[r:r + NORM_ROWS, :] = x * lax.rsqrt(ms + EPS) * fg


def _ffn(hn, h, mods3, row0, w_gate, w_up, w_down, final_g, *, tm, th, rows, ncols):
    bx, lx, d = h.shape
    final = final_g is not None
    row = lambda b, i: (b, i, 0)
    any_spec = pl.BlockSpec(memory_space=pl.ANY)
    in_specs = [pl.BlockSpec((None, tm, d), row), any_spec,
                pl.BlockSpec((None, 1, d), _mod_spec(row0, 5)),
                any_spec, any_spec, any_spec]
    args = [hn, h.reshape(bx * lx, d), mods3, w_gate, w_up, w_down]
    if final:
        in_specs.append(_const_spec((1, d)))
        args.append(final_g.reshape(1, d))
    return pl.pallas_call(
        functools.partial(_ffn_kernel, final=final, th=th, rows=rows, ncols=ncols),
        grid=(bx, lx // tm),
        in_specs=in_specs,
        out_specs=pl.BlockSpec((tm, d), lambda b, i: (b * (lx // tm) + i, 0)),
        out_shape=jax.ShapeDtypeStruct((bx * lx, d), F32),
        scratch_shapes=[pltpu.VMEM((2, d, th), BF16), pltpu.VMEM((2, d, th), BF16), pltpu.VMEM((2, th, d), BF16),
                        pltpu.VMEM((tm, d), F32), pltpu.SemaphoreType.DMA((3, 2)), pltpu.SemaphoreType.DMA(())],
        name="ffn",
        compiler_params=_params(("arbitrary", "arbitrary")),
    )(*args).reshape(bx, lx, d)


def _rope_tables(n):
    pos = np.arange(n)
    inv = np.power(ROPE_THETA, -np.arange(0, AXIS_DIM, 2, dtype=np.float64) / AXIS_DIM)
    ang_r = (pos // GRID_W)[:, None] * inv
    ang_c = (pos % GRID_W)[:, None] * inv
    cos = np.concatenate([np.cos(ang_r), np.cos(ang_r), np.cos(ang_c), np.cos(ang_c)], axis=-1)
    sin = np.concatenate([-np.sin(ang_r), np.sin(ang_r), -np.sin(ang_c), np.sin(ang_c)], axis=-1)
    return jnp.asarray(cos, F32), jnp.asarray(sin, F32)


def kernel(x, c, ctx, c_ctx, l0_norm1_g, l0_w_mod, l0_b_mod, l0_w_in, l0_q_norm_g, l0_k_norm_g, l0_conv_w, l0_w_out, l0_norm2_g, l0_w_gate, l0_w_up, l0_w_down, l1_norm1_g, l1_w_mod, l1_b_mod, l1_w_in, l1_q_norm_g, l1_k_norm_g, l1_sink, l1_pool_w, l1_pool_scale, l1_w_out, l1_norm2_g, l1_w_gate, l1_w_up, l1_w_down, final_norm_g):
    b, s, d = x.shape
    lc = ctx.shape[1]
    rope = _rope_tables(s)
    c_all = jnp.zeros((MOD_ROWS, d), F32).at[:b].set(c).at[b].set(c_ctx)
    ctx_row = b
    bf = lambda w: w.astype(BF16)
    tm_in, tm_out = 512, 512
    ffn_tiles = dict(tm=1024, th=512, rows=512, ncols=512)

    mods = _modulation(c_all, l0_w_mod, l0_b_mod).reshape(MOD_ROWS, 1, N_MOD * d)
    w_in = bf(l0_w_in)
    q_g, k_g = l0_q_norm_g, l0_k_norm_g
    hc = ctx.reshape(1, b * lc, d)

    q, k, v, aux = _inproj(x, mods, 0, l0_norm1_g, w_in, q_g, k_g, rope, tm=tm_in)
    qc, kc, vc, auxc = _inproj(hc, mods, ctx_row, l0_norm1_g, w_in, q_g, k_g, None, tm=tm_in)
    qc, kc, vc = (t.reshape(b, lc, -1) for t in (qc, kc, vc))

    later = (l0_w_out, l0_w_gate, l0_w_up, l0_w_down, l1_w_in, l1_w_out, l1_w_gate, l1_w_up, l1_w_down)
    attn, casted = _attention(q, k, v, kc, vc, None, mode="dense", tq=256, nsub=2, tk=512,
                              cast=[w.reshape(-1, CAST_COLS) for w in later])
    (w_out, w_gate, w_up, w_down, w_in1, w_out1, w_gate1, w_up1, w_down1) = (
        cw.reshape(w.shape) for cw, w in zip(casted, later))
    h, hn = _outproj(attn, aux, "conv", (l0_conv_w,), s, w_out, x, mods, 0, l0_norm2_g, tm=tm_out)
    h = _ffn(hn, h, mods, 0, w_gate, w_up, w_down, None, **ffn_tiles)

    attn_c = _attention(qc, None, None, kc, vc, None, mode="ctx", tq=lc, nsub=1, tk=lc)
    hc, hcn = _outproj(attn_c.reshape(1, b * lc, -1), auxc, "conv", (l0_conv_w,), lc, w_out, hc, mods, ctx_row,
                       l0_norm2_g, tm=tm_out)
    hc = _ffn(hcn, hc, mods, ctx_row, w_gate, w_up, w_down, None, **ffn_tiles)

    mods = _modulation(c_all, l1_w_mod, l1_b_mod).reshape(MOD_ROWS, 1, N_MOD * d)
    w_in, w_out, w_gate, w_up, w_down = w_in1, w_out1, w_gate1, w_up1, w_down1
    q_g, k_g = l1_q_norm_g, l1_k_norm_g

    q, k, v, u = _inproj(h, mods, 0, l1_norm1_g, w_in, q_g, k_g, rope, tm=tm_in)
    kc, vc = _inproj(hc, mods, ctx_row, l1_norm1_g, w_in, q_g, k_g, None, tm=tm_in,
                     want_q=False, want_aux=False)
    kc, vc = kc.reshape(b, lc, -1), vc.reshape(b, lc, -1)

    attn = _attention(q, k, v, kc, vc, l1_sink, mode="window", tq=256, nsub=8, tk=512)
    pool_params = (bf(l1_pool_w), l1_pool_scale.reshape(1, -1))
    h, hn = _outproj(attn, u, "pool", pool_params, s, w_out, h, mods, 0, l1_norm2_g, tm=tm_out)
    return _ffn(hn, h, mods, 0, w_gate, w_up, w_down, final_norm_g, **ffn_tiles)
```

```python
import functools
import math

import jax
import jax.numpy as jnp
import numpy as np
from jax import lax
from jax.experimental import pallas as pl
from jax.experimental.pallas import tpu as pltpu

F32 = jnp.float32
BF16 = jnp.bfloat16

LANES = 128
HEAD_DIM = 128
AXIS_DIM = HEAD_DIM // 2
Q_HEADS = 12
KV_HEADS = 4
GROUPS = Q_HEADS // KV_HEADS
ATT_WIDTH = Q_HEADS * HEAD_DIM
KV_WIDTH = KV_HEADS * HEAD_DIM
GRID_W = 64
WINDOW = 128
POOL_WINDOWS = (2, 4, 8, 16)
ROPE_THETA = 10000.0
N_MOD = 6
EPS = 1e-6
NEG_INF = -1e30
ATT_SCALE = HEAD_DIM ** -0.5
LOG2E = math.log2(math.e)
MOD_ROWS = 8
VMEM_LIMIT = 56 * 1024 * 1024
NORM_ROWS = 16
CAST_CHUNK_ROWS = {2048: 512, 3072: 256, 5632: 128}
HALO = 8
SCORE_LOOKAHEAD = 2
VT_ROWS = HEAD_DIM + 16


def _params(sem):
    return pltpu.CompilerParams(dimension_semantics=sem, vmem_limit_bytes=VMEM_LIMIT)


def _mod_spec(row0, chunk):
    def imap(b, *_):
        return (b + row0, 0, chunk)
    return imap


def _const_spec(shape):
    zeros = (0,) * len(shape)
    return pl.BlockSpec(shape, lambda *_: zeros, pipeline_mode=pl.Buffered(1))


def _mod_kernel(c_ref, w_ref, b_ref, o_ref):
    c = c_ref[...]
    sc = c * jax.nn.sigmoid(c)
    o_ref[...] = jnp.dot(sc.astype(BF16), w_ref[...].astype(BF16),
                         preferred_element_type=F32) + b_ref[...]


def _modulation(c_all, w_mod, b_mod):
    d, n = w_mod.shape
    tn = 1024
    return pl.pallas_call(
        _mod_kernel,
        grid=(n // tn,),
        in_specs=[pl.BlockSpec((MOD_ROWS, d), lambda j: (0, 0)),
                  pl.BlockSpec((d, tn), lambda j: (0, j)),
                  pl.BlockSpec((1, tn), lambda j: (0, j))],
        out_specs=pl.BlockSpec((MOD_ROWS, tn), lambda j: (0, j)),
        out_shape=jax.ShapeDtypeStruct((MOD_ROWS, n), F32),
        name="modulation",
        compiler_params=_params(("arbitrary",)),
    )(c_all, w_mod, b_mod.reshape(1, n))


def _rms_modulate_rows(x, gs, shift):
    ms = jnp.mean(x * x, axis=-1, keepdims=True)
    return x * lax.rsqrt(ms + EPS) * gs + shift


def _norm_to(dst_ref, src, gs, shift, row0=0):
    rows = src.shape[0]
    for r in range(0, rows, NORM_ROWS):
        x = src[r:r + NORM_ROWS, :]
        dst_ref[row0 + r:row0 + r + NORM_ROWS, :] = _rms_modulate_rows(x, gs, shift).astype(BF16)


def _head_norm_rope(y, g, cos, sin):
    lane = lax.broadcasted_iota(jnp.int32, (1, HEAD_DIM), 1)
    in_x1 = (lane & (AXIS_DIM // 2)) == 0
    outs = []
    for hh in range(y.shape[1] // HEAD_DIM):
        t = y[:, hh * HEAD_DIM:(hh + 1) * HEAD_DIM]
        ms = jnp.mean(t * t, axis=-1, keepdims=True)
        t = t * lax.rsqrt(ms + EPS) * g
        if cos is not None:
            partner = jnp.where(in_x1, pltpu.roll(t, HEAD_DIM - AXIS_DIM // 2, 1), pltpu.roll(t, AXIS_DIM // 2, 1))
            t = t * cos + partner * sin
        outs.append(t)
    return jnp.concatenate(outs, axis=1)


def _inproj_kernel(*refs, tn, nq, rows, rope, want_q, want_aux):
    it = iter(refs)
    h_ref, shift_ref, scale_ref, g1_ref, wqk_ref, wr_ref, qg_ref, kg_ref = (next(it) for _ in range(8))
    cos_ref = sin_ref = None
    if rope:
        cos_ref, sin_ref = next(it), next(it)
    q_ref = next(it) if want_q else None
    k_ref, v_ref = next(it), next(it)
    aux_ref = next(it) if want_aux else None
    xn_ref = next(it)

    gs = g1_ref[...] * (1.0 + scale_ref[...])
    shift = shift_ref[...]
    qg = qg_ref[...] * (ATT_SCALE * LOG2E)
    kg = kg_ref[...]

    for r in range(0, h_ref.shape[0], rows):
        rs = slice(r, r + rows)
        _norm_to(xn_ref, h_ref.at[rs, :], gs, shift, row0=r)
        cos = cos_ref[rs, :] if rope else None
        sin = sin_ref[rs, :] if rope else None

        def tile(w_ref, j):
            return jnp.dot(xn_ref[rs, :], w_ref[:, j * tn:(j + 1) * tn], preferred_element_type=F32)

        if want_q:
            for j in range(nq):
                q_ref[rs, j * tn:(j + 1) * tn] = _head_norm_rope(tile(wqk_ref, j), qg, cos, sin).astype(BF16)
        k_ref[rs, :] = _head_norm_rope(tile(wqk_ref, nq), kg, cos, sin).astype(BF16)
        v_ref[rs, :] = tile(wr_ref, 0).astype(BF16)
        if want_aux:
            for j in range(aux_ref.shape[1] // tn):
                aux_ref[rs, j * tn:(j + 1) * tn] = tile(wr_ref, 1 + j)


def _inproj(h, mods3, row0, g1, w_in, q_g, k_g, rope_tabs, *, tm, want_q=True, want_aux=True):
    bx, lx, d = h.shape
    tn = KV_WIDTH
    nq = ATT_WIDTH // tn
    rope = rope_tabs is not None
    nqk = ATT_WIDTH + KV_WIDTH
    n_rest = w_in.shape[1] - nqk
    assert nqk % n_rest == 0
    aux_w = n_rest - KV_WIDTH
    row = lambda b, i: (b, i, 0)

    in_specs = [
        pl.BlockSpec((None, tm, d), row),
        pl.BlockSpec((None, 1, d), _mod_spec(row0, 0)),
        pl.BlockSpec((None, 1, d), _mod_spec(row0, 1)),
        _const_spec((1, d)),
        pl.BlockSpec((d, nqk), lambda b, i: (0, 0), pipeline_mode=pl.Buffered(1)),
        pl.BlockSpec((d, n_rest), lambda b, i: (0, nqk // n_rest), pipeline_mode=pl.Buffered(1)),
        _const_spec((1, HEAD_DIM)),
        _const_spec((1, HEAD_DIM)),
    ]
    args = [h, mods3, mods3, g1.reshape(1, d), w_in, w_in, q_g.reshape(1, HEAD_DIM), k_g.reshape(1, HEAD_DIM)]
    if rope:
        in_specs += [pl.BlockSpec((tm, HEAD_DIM), lambda b, i: (i, 0))] * 2
        args += list(rope_tabs)

    out_specs, out_shape = [], []
    if want_q:
        out_specs.append(pl.BlockSpec((None, tm, ATT_WIDTH), row))
        out_shape.append(jax.ShapeDtypeStruct((bx, lx, ATT_WIDTH), BF16))
    for _ in range(2):
        out_specs.append(pl.BlockSpec((None, tm, KV_WIDTH), row))
        out_shape.append(jax.ShapeDtypeStruct((bx, lx, KV_WIDTH), BF16))
    if want_aux:
        out_specs.append(pl.BlockSpec((None, tm, aux_w), row))
        out_shape.append(jax.ShapeDtypeStruct((bx, lx, aux_w), F32))

    return pl.pallas_call(
        functools.partial(_inproj_kernel, tn=tn, nq=nq, rows=tm // 2, rope=rope, want_q=want_q,
                          want_aux=want_aux),
        grid=(bx, lx // tm),
        in_specs=in_specs,
        out_specs=out_specs,
        out_shape=out_shape,
        scratch_shapes=[pltpu.VMEM((tm, d), BF16)],
        name="inproj",
        compiler_params=_params(("arbitrary", "arbitrary")),
    )(*args)


def _transpose_bf16(x):
    eye = (lax.broadcasted_iota(jnp.int32, (HEAD_DIM, HEAD_DIM), 0)
           == lax.broadcasted_iota(jnp.int32, (HEAD_DIM, HEAD_DIM), 1)).astype(BF16)
    return lax.dot_general(eye, x, (((1,), (1,)), ((), ())), preferred_element_type=F32).astype(BF16)


def _scores(kch, q, mask):
    s = lax.dot_general(kch, q, (((1,), (1,)), ((), ())), preferred_element_type=F32)
    return s if mask is None else jnp.where(mask, s, NEG_INF)


def _softmax_step(state, s, vt):
    m_prev, acc = state
    m_next = jnp.maximum(m_prev, jnp.max(s, axis=0, keepdims=True))
    p = jnp.exp2(s - m_next).astype(BF16)
    alpha = jnp.exp2(m_prev - m_next)
    acc = alpha * acc + jnp.dot(vt, p, preferred_element_type=F32)
    return m_next, acc


def _cast_classes(weights):
    cols = sorted({w.shape[1] for w in weights})
    return cols, [cols.index(w.shape[1]) for w in weights]


def _cast_job(t, w_f32, w_bf16, bufs_in, bufs_out, sem_in, sem_out):
    cols, cls = _cast_classes(w_f32)
    rows = [CAST_CHUNK_ROWS[w.shape[1]] for w in w_f32]
    counts = [w.shape[0] // r for w, r in zip(w_f32, rows)]
    first = [sum(counts[:a]) for a in range(len(counts))]
    slot = lax.rem(t, 2)

    def chunk(ref, a, c):
        return ref.at[pl.ds(pl.multiple_of(c * rows[a], rows[a]), rows[a]), :]

    def fetch(a, c, sl):
        return pltpu.make_async_copy(chunk(w_f32[a], a, c), bufs_in[cls[a]].at[sl], sem_in.at[cls[a], sl])

    def store(a, c, sl):
        return pltpu.make_async_copy(bufs_out[cls[a]].at[sl], chunk(w_bf16[a], a, c), sem_out.at[cls[a], sl])

    def is_job_of(tt, a):
        return (tt >= first[a]) & (tt < first[a] + counts[a])

    @pl.when(t == 0)
    def _():
        fetch(0, 0, 0).start()

    for a in range(len(counts)):
        @pl.when(is_job_of(t + 1, a))
        def _(a=a):
            fetch(a, t + 1 - first[a], 1 - slot).start()

        @pl.when(is_job_of(t - 2, a))
        def _(a=a):
            store(a, t - 2 - first[a], slot).wait()

        @pl.when(is_job_of(t, a))
        def _(a=a):
            fetch(a, t - first[a], slot).wait()
            bufs_out[cls[a]][slot] = bufs_in[cls[a]][slot].astype(BF16)
            store(a, t - first[a], slot).start()


def _attn_kernel(*refs, mode, tq, nsub, tk, seq, has_sink, n_cast):
    it = iter(refs)
    q_ref = next(it)
    k_ref = v_ref = None
    if mode != "ctx":
        k_ref, v_ref = next(it), next(it)
    kc_ref, vc_ref = next(it), next(it)
    sink_ref = next(it) if has_sink else None
    cast_in = [next(it) for _ in range(n_cast)]
    o_ref = next(it)
    cast_out = [next(it) for _ in range(n_cast)]
    vt_ref = next(it) if mode != "ctx" else None
    vct_ref = next(it)

    kvh = pl.program_id(1)
    step = pl.program_id(2)
    if n_cast:
        n_cls = len(_cast_classes(cast_in)[0])
        bufs_in = [next(it) for _ in range(n_cls)]
        bufs_out = [next(it) for _ in range(n_cls)]
        t = (pl.program_id(0) * pl.num_programs(1) + kvh) * pl.num_programs(2) + step
        _cast_job(t, cast_in, cast_out, bufs_in, bufs_out, next(it), next(it))

    @pl.when(step == 0)
    def _():
        vct_ref[:HEAD_DIM, :] = _transpose_bf16(vc_ref[...])
        vct_ref[HEAD_DIM:, :] = jnp.ones((VT_ROWS - HEAD_DIM, vct_ref.shape[1]), BF16)
        if mode != "ctx":
            vt_ref[HEAD_DIM:, :] = jnp.ones((VT_ROWS - HEAD_DIM, seq), BF16)
            for c in range(0, seq, tk):
                vt_ref[:HEAD_DIM, c:c + tk] = _transpose_bf16(v_ref[c:c + tk, :])

    n = GROUPS * tq
    if mode == "window":
        band0 = (lax.broadcasted_iota(jnp.int32, (tk, n), 0)
                 - (lax.broadcasted_iota(jnp.int32, (tk, n), 1) & (tq - 1)) + WINDOW)
        interior_mask = (band0 - WINDOW).astype(jnp.uint32) <= 2 * WINDOW

    items = []
    for sub in range(nsub):
        start = (step * nsub + sub) * tq
        rows = slice(sub * tq, (sub + 1) * tq)
        q = jnp.concatenate([q_ref[rows, g * HEAD_DIM:(g + 1) * HEAD_DIM] for g in range(GROUPS)], axis=0)
        if mode == "dense":
            for c in range(0, seq, tk):
                items.append((sub, q, None, (lambda c=c: k_ref[c:c + tk, :]), (lambda c=c: vt_ref[:, c:c + tk])))
        elif mode == "window":
            ks = pl.multiple_of(jnp.clip(start - WINDOW, 0, seq - tk), WINDOW)
            if 0 < sub < nsub - 1:
                mask = interior_mask
            else:
                mask = (band0 + (ks - start)).astype(jnp.uint32) <= 2 * WINDOW
            items.append((sub, q, mask, (lambda ks=ks: k_ref[pl.ds(ks, tk), :]),
                          (lambda ks=ks: vt_ref[:, pl.ds(ks, tk)])))
        items.append((sub, q, None, (lambda: kc_ref[...]), (lambda: vct_ref[...])))

    per_sub = len(items) // nsub
    items = [items[sub * per_sub + c] for c in range(per_sub) for sub in range(nsub)]

    states = [(jnp.full((1, n), NEG_INF, F32), jnp.zeros((VT_ROWS, n), F32)) for _ in range(nsub)]
    score = lambda item: _scores(item[3](), item[1], item[2])
    pending = [score(item) for item in items[:SCORE_LOOKAHEAD]]
    for idx, (sub, _, _, _, vt_fn) in enumerate(items):
        s_cur = pending.pop(0)
        if idx + SCORE_LOOKAHEAD < len(items):
            pending.append(score(items[idx + SCORE_LOOKAHEAD]))
        states[sub] = _softmax_step(states[sub], s_cur, vt_fn())

    for sub, (m, acc) in enumerate(states):
        l = acc[HEAD_DIM:HEAD_DIM + 1, :]
        acc = acc[:HEAD_DIM, :]
        if has_sink:
            sink = jnp.concatenate([jnp.full((1, tq), sink_ref[kvh * GROUPS + g] * LOG2E, F32)
                                    for g in range(GROUPS)], axis=1)
            l = l + jnp.exp2(sink - m)
        out = acc / l
        for g in range(GROUPS):
            o_ref[sub * tq:(sub + 1) * tq, g * HEAD_DIM:(g + 1) * HEAD_DIM] = (
                out[:, g * tq:(g + 1) * tq].T.astype(BF16))


def _attention(q, k, v, kc, vc, sink, *, mode, tq, nsub, tk, cast=()):
    b, lq, _ = q.shape
    lc = kc.shape[1]
    seq = k.shape[1] if k is not None else 0
    gw = GROUPS * HEAD_DIM
    tstep = tq * nsub
    in_specs = [pl.BlockSpec((None, tstep, gw), lambda bb, h, i: (bb, i, h))]
    args = [q]
    scratch = []
    if mode != "ctx":
        in_specs += [pl.BlockSpec((None, seq, HEAD_DIM), lambda bb, h, i: (bb, 0, h))] * 2
        args += [k, v]
        scratch.append(pltpu.VMEM((VT_ROWS, seq), BF16))
    in_specs += [pl.BlockSpec((None, lc, HEAD_DIM), lambda bb, h, i: (bb, 0, h))] * 2
    args += [kc, vc]
    scratch.append(pltpu.VMEM((VT_ROWS, lc), BF16))
    has_sink = sink is not None
    if has_sink:
        in_specs.append(pl.BlockSpec(memory_space=pltpu.SMEM))
        args.append(sink)
    out_specs = [pl.BlockSpec((None, tstep, gw), lambda bb, h, i: (bb, i, h))]
    out_shape = [jax.ShapeDtypeStruct((b, lq, ATT_WIDTH), BF16)]
    if cast:
        n_steps = b * KV_HEADS * (lq // tstep)
        n_jobs = sum(w.shape[0] // CAST_CHUNK_ROWS[w.shape[1]] for w in cast)
        assert n_jobs + 2 <= n_steps
        cols = _cast_classes(cast)[0]
        in_specs += [pl.BlockSpec(memory_space=pl.ANY)] * len(cast)
        args += list(cast)
        out_specs += [pl.BlockSpec(memory_space=pl.ANY)] * len(cast)
        out_shape += [jax.ShapeDtypeStruct(w.shape, BF16) for w in cast]
        scratch += [pltpu.VMEM((2, CAST_CHUNK_ROWS[c], c), F32) for c in cols]
        scratch += [pltpu.VMEM((2, CAST_CHUNK_ROWS[c], c), BF16) for c in cols]
        scratch += [pltpu.SemaphoreType.DMA((len(cols), 2)), pltpu.SemaphoreType.DMA((len(cols), 2))]
    outs = pl.pallas_call(
        functools.partial(_attn_kernel, mode=mode, tq=tq, nsub=nsub, tk=tk, seq=seq, has_sink=has_sink,
                          n_cast=len(cast)),
        grid=(b, KV_HEADS, lq // tstep),
        in_specs=in_specs,
        out_specs=out_specs,
        out_shape=out_shape,
        scratch_shapes=scratch,
        name="attn_" + mode,
        compiler_params=_params(("arbitrary", "arbitrary", "arbitrary")),
    )(*args)
    return (outs[0], outs[1:]) if cast else outs[0]


def _conv_mixer(g, aux_ref, prev_ref, next_ref, w_ref, row0, seq_len):
    tm, w = aux_ref.shape[0], aux_ref.shape[1] // 3
    gb, gc, uu = (slice(k * w + g * LANES, k * w + (g + 1) * LANES) for k in range(3))
    z = aux_ref[:, gc] * aux_ref[:, uu]
    z_prev = prev_ref[HALO - 1:HALO, gc] * prev_ref[HALO - 1:HALO, uu]
    z_next = next_ref[0:1, gc] * next_ref[0:1, uu]
    row = lax.broadcasted_iota(jnp.int32, (tm, LANES), 0)
    pos = (row0 + row) & (seq_len - 1)
    down = jnp.where(row == 0, z_prev, pltpu.roll(z, 1, 0))
    down = jnp.where(pos == 0, 0.0, down)
    up = jnp.where(row == tm - 1, z_next, pltpu.roll(z, tm - 1, 0))
    up = jnp.where(pos == seq_len - 1, 0.0, up)
    cw = w_ref[:, g * LANES:(g + 1) * LANES]
    return aux_ref[:, gb] * (cw[0:1, :] * down + cw[1:2, :] * z + cw[2:3, :] * up)


def _pool_mixer(g, u_ref, prev_ref, next_ref, w_ref, s_ref, row0, seq_len):
    tm = u_ref.shape[0]
    at_start = (row0 & (seq_len - 1)) == 0
    at_end = ((row0 + tm) & (seq_len - 1)) == 0
    pos = (row0 & (seq_len - 1)) + lax.broadcasted_iota(jnp.int32, (tm, LANES), 0)
    cols = slice(g * LANES, (g + 1) * LANES)
    u = u_ref[:, cols]
    ext = jnp.concatenate([jnp.where(at_start, 0.0, prev_ref[:, cols]), u,
                           jnp.where(at_end, 0.0, next_ref[:, cols])], axis=0)
    n = ext.shape[0]
    half = POOL_WINDOWS[g] // 2
    fwd = ext
    bwd = pltpu.roll(ext, 1, 0)
    k = 1
    while k < half:
        fwd = fwd + pltpu.roll(fwd, n - k, 0)
        bwd = bwd + pltpu.roll(bwd, k, 0)
        k *= 2
    cnt = jnp.minimum(pos + half, seq_len) - jnp.maximum(pos - half, 0)
    pooled = (fwd + bwd)[HALO:HALO + tm, :] / cnt.astype(F32) - u
    mixed = jnp.dot(pooled.astype(BF16), w_ref[g], preferred_element_type=F32)
    return mixed * s_ref[:, cols]


def _outproj_kernel(*refs, mixer, rows, seq_len):
    it = iter(refs)
    a_ref, aux_ref, prev_ref, next_ref = (next(it) for _ in range(4))
    mix_refs = [next(it) for _ in range(1 if mixer == "conv" else 2)]
    wa_ref, wx_ref, h_ref, gate_ref, shift_ref, scale_ref, g2_ref, o_ref, hn_ref, x_ref = it

    row0 = pl.program_id(1) * a_ref.shape[0]
    mix = _conv_mixer if mixer == "conv" else _pool_mixer
    gate = gate_ref[...]
    gs = g2_ref[...] * (1.0 + scale_ref[...])
    shift = shift_ref[...]
    n_groups = x_ref.shape[1] // LANES
    ncols = o_ref.shape[1] // n_groups
    for r in range(0, a_ref.shape[0], rows):
        rs = slice(r, r + rows)
        for g in range(n_groups):
            cs = slice(g * ncols, (g + 1) * ncols)
            o_ref[rs, cs] = h_ref[rs, cs] + gate[:, cs] * jnp.dot(a_ref[rs, :], wa_ref[:, cs],
                                                               preferred_element_type=F32)
            if r == 0:
                x_ref[:, g * LANES:(g + 1) * LANES] = mix(g, aux_ref, prev_ref, next_ref, *mix_refs,
                                                          row0, seq_len).astype(BF16)
        o_ref[rs, :] += gate * jnp.dot(x_ref[rs, :], wx_ref[...], preferred_element_type=F32)
        _norm_to(hn_ref, o_ref.at[rs, :], gs, shift, row0=r)


def _outproj(attn, aux, mixer, mix_params, seq_len, w_out, h, mods3, row0, g2, *, tm):
    bx, lx, d = h.shape
    wa, aux_w = attn.shape[2], aux.shape[2]
    wx = d - wa
    row = lambda b, i: (b, i, 0)
    halo_blocks = tm // HALO
    mix_specs = [_const_spec(p.shape) for p in mix_params]
    return pl.pallas_call(
        functools.partial(_outproj_kernel, mixer=mixer, rows=tm // 2, seq_len=seq_len),
        grid=(bx, lx // tm),
        in_specs=[pl.BlockSpec((None, tm, wa), row),
                  pl.BlockSpec((None, tm, aux_w), row),
                  pl.BlockSpec((None, HALO, aux_w), lambda b, i: (b, jnp.maximum(i * halo_blocks - 1, 0), 0)),
                  pl.BlockSpec((None, HALO, aux_w),
                               lambda b, i: (b, jnp.minimum((i + 1) * halo_blocks, lx // HALO - 1), 0)),
                  *mix_specs,
                  pl.BlockSpec((wa, d), lambda b, i: (0, 0), pipeline_mode=pl.Buffered(1)),
                  pl.BlockSpec((wx, d), lambda b, i: (wa // wx, 0), pipeline_mode=pl.Buffered(1)),
                  pl.BlockSpec((None, tm, d), row),
                  pl.BlockSpec((None, 1, d), _mod_spec(row0, 2)),
                  pl.BlockSpec((None, 1, d), _mod_spec(row0, 3)),
                  pl.BlockSpec((None, 1, d), _mod_spec(row0, 4)),
                  _const_spec((1, d))],
        out_specs=[pl.BlockSpec((None, tm, d), row), pl.BlockSpec((None, tm, d), row)],
        out_shape=[jax.ShapeDtypeStruct((bx, lx, d), F32), jax.ShapeDtypeStruct((bx, lx, d), BF16)],
        scratch_shapes=[pltpu.VMEM((tm, wx), BF16)],
        name="outproj_" + mixer,
        compiler_params=_params(("arbitrary", "arbitrary")),
    )(attn, aux, aux, aux, *mix_params, w_out, w_out, h, mods3, mods3, mods3, g2.reshape(1, d))


def _ffn_kernel(*refs, final, th, rows, ncols):
    it = iter(refs)
    hn_ref, h_hbm, gate_ref, wg_hbm, wu_hbm, wd_hbm = (next(it) for _ in range(6))
    fg_ref = next(it) if final else None
    o_ref, wg_buf, wu_buf, wd_buf, h_buf, w_sem, h_sem = (next(it) for _ in range(7))

    b, i = pl.program_id(0), pl.program_id(1)
    tm, d = o_ref.shape
    n_hid = wg_hbm.shape[1] // th
    tile = b * pl.num_programs(1) + i
    n_steps = pl.num_programs(0) * pl.num_programs(1) * n_hid
    step0 = tile * n_hid

    def w_copies(j, slot):
        col = j * th if isinstance(j, int) else pl.multiple_of(j * th, th)
        return (pltpu.make_async_copy(wg_hbm.at[:, pl.ds(col, th)], wg_buf.at[slot], w_sem.at[0, slot]),
                pltpu.make_async_copy(wu_hbm.at[:, pl.ds(col, th)], wu_buf.at[slot], w_sem.at[1, slot]),
                pltpu.make_async_copy(wd_hbm.at[pl.ds(col, th), :], wd_buf.at[slot], w_sem.at[2, slot]))

    h_copy = pltpu.make_async_copy(h_hbm.at[pl.ds(pl.multiple_of(tile * tm, tm), tm), :], h_buf, h_sem)
    h_copy.start()

    @pl.when(tile == 0)
    def _():
        for cp in w_copies(0, 0):
            cp.start()

    gate = gate_ref[...]

    def hidden_step(j, first=False, last=False):
        step = step0 + j
        slot = lax.rem(step, 2)
        for cp in w_copies(j, slot):
            cp.wait()

        @pl.when(step + 1 < n_steps)
        def _():
            nxt = jnp.where(j + 1 < n_hid, j + 1, 0)
            for cp in w_copies(nxt, 1 - slot):
                cp.start()

        wg, wu, wd = wg_buf.at[slot], wu_buf.at[slot], wd_buf.at[slot]
        for r in range(0, tm, rows):
            rs = slice(r, r + rows)
            hn = hn_ref[rs, :]
            gt = jnp.dot(hn, wg[...], preferred_element_type=F32)
            up = jnp.dot(hn, wu[...], preferred_element_type=F32)
            act = (gt * jax.nn.sigmoid(gt) * up).astype(BF16)
            if last and r == 0:
                h_copy.wait()
            for c in range(0, d, ncols):
                cs = slice(c, c + ncols)
                dn = gate[:, cs] * jnp.dot(act, wd[:, cs], preferred_element_type=F32)
                if first:
                    o_ref[rs, cs] = dn
                elif last:
                    o_ref[rs, cs] += dn + h_buf[rs, cs]
                else:
                    o_ref[rs, cs] += dn

    hidden_step(0, first=True)

    def loop_body(j, carry):
        hidden_step(j)
        return carry

    lax.fori_loop(1, n_hid - 1, loop_body, 0)
    hidden_step(n_hid - 1, last=True)

    if final:
        fg = fg_ref[...]
        for r in range(0, tm, NORM_ROWS):
            x = o_ref[r:r + NORM_ROWS, :]
            ms = jnp.mean(x * x, axis=-1, keepdims=True)
            o_ref[r:r + NORM_ROWS, :] = x * lax.rsqrt(ms + EPS) * fg


def _ffn(hn, h, mods3, row0, w_gate, w_up, w_down, final_g, *, tm, th, rows, ncols):
    bx, lx, d = h.shape
    final = final_g is not None
    row = lambda b, i: (b, i, 0)
    any_spec = pl.BlockSpec(memory_space=pl.ANY)
    in_specs = [pl.BlockSpec((None, tm, d), row), any_spec,
                pl.BlockSpec((None, 1, d), _mod_spec(row0, 5)),
                any_spec, any_spec, any_spec]
    args = [hn, h.reshape(bx * lx, d), mods3, w_gate, w_up, w_down]
    if final:
        in_specs.append(_const_spec((1, d)))
        args.append(final_g.reshape(1, d))
    return pl.pallas_call(
        functools.partial(_ffn_kernel, final=final, th=th, rows=rows, ncols=ncols),
        grid=(bx, lx // tm),
        in_specs=in_specs,
        out_specs=pl.BlockSpec((tm, d), lambda b, i: (b * (lx // tm) + i, 0)),
        out_shape=jax.ShapeDtypeStruct((bx * lx, d), F32),
        scratch_shapes=[pltpu.VMEM((2, d, th), BF16), pltpu.VMEM((2, d, th), BF16), pltpu.VMEM((2, th, d), BF16),
                        pltpu.VMEM((tm, d), F32), pltpu.SemaphoreType.DMA((3, 2)), pltpu.SemaphoreType.DMA(())],
        name="ffn",
        compiler_params=_params(("arbitrary", "arbitrary")),
    )(*args).reshape(bx, lx, d)


def _rope_tables(n):
    pos = np.arange(n)
    inv = np.power(ROPE_THETA, -np.arange(0, AXIS_DIM, 2, dtype=np.float64) / AXIS_DIM)
    ang_r = (pos // GRID_W)[:, None] * inv
    ang_c = (pos % GRID_W)[:, None] * inv
    cos = np.concatenate([np.cos(ang_r), np.cos(ang_r), np.cos(ang_c), np.cos(ang_c)], axis=-1)
    sin = np.concatenate([-np.sin(ang_r), np.sin(ang_r), -np.sin(ang_c), np.sin(ang_c)], axis=-1)
    return jnp.asarray(cos, F32), jnp.asarray(sin, F32)


def kernel(x, c, ctx, c_ctx, l0_norm1_g, l0_w_mod, l0_b_mod, l0_w_in, l0_q_norm_g, l0_k_norm_g, l0_conv_w, l0_w_out, l0_norm2_g, l0_w_gate, l0_w_up, l0_w_down, l1_norm1_g, l1_w_mod, l1_b_mod, l1_w_in, l1_q_norm_g, l1_k_norm_g, l1_sink, l1_pool_w, l1_pool_scale, l1_w_out, l1_norm2_g, l1_w_gate, l1_w_up, l1_w_down, final_norm_g):
    b, s, d = x.shape
    lc = ctx.shape[1]
    rope = _rope_tables(s)
    c_all = jnp.zeros((MOD_ROWS, d), F32).at[:b].set(c).at[b].set(c_ctx)
    ctx_row = b
    bf = lambda w: w.astype(BF16)
    tm_in, tm_out = 512, 512
    ffn_tiles = dict(tm=1024, th=512, rows=512, ncols=512)

    mods = _modulation(c_all, l0_w_mod, l0_b_mod).reshape(MOD_ROWS, 1, N_MOD * d)
    w_in = bf(l0_w_in)
    q_g, k_g = l0_q_norm_g, l0_k_norm_g
    hc = ctx.reshape(1, b * lc, d)

    q, k, v, aux = _inproj(x, mods, 0, l0_norm1_g, w_in, q_g, k_g, rope, tm=tm_in)
    qc, kc, vc, auxc = _inproj(hc, mods, ctx_row, l0_norm1_g, w_in, q_g, k_g, None, tm=tm_in)
    qc, kc, vc = (t.reshape(b, lc, -1) for t in (qc, kc, vc))

    later = (l0_w_out, l0_w_gate, l0_w_up, l0_w_down, l1_w_in, l1_w_out, l1_w_gate, l1_w_up, l1_w_down)
    attn, casted = _attention(q, k, v, kc, vc, None, mode="dense", tq=256, nsub=2, tk=512, cast=later)
    w_out, w_gate, w_up, w_down, w_in1, w_out1, w_gate1, w_up1, w_down1 = casted
    h, hn = _outproj(attn, aux, "conv", (l0_conv_w,), s, w_out, x, mods, 0, l0_norm2_g, tm=tm_out)
    h = _ffn(hn, h, mods, 0, w_gate, w_up, w_down, None, **ffn_tiles)

    attn_c = _attention(qc, None, None, kc, vc, None, mode="ctx", tq=lc, nsub=1, tk=lc)
    hc, hcn = _outproj(attn_c.reshape(1, b * lc, -1), auxc, "conv", (l0_conv_w,), lc, w_out, hc, mods, ctx_row,
                       l0_norm2_g, tm=tm_out)
    hc = _ffn(hcn, hc, mods, ctx_row, w_gate, w_up, w_down, None, **ffn_tiles)

    mods = _modulation(c_all, l1_w_mod, l1_b_mod).reshape(MOD_ROWS, 1, N_MOD * d)
    w_in, w_out, w_gate, w_up, w_down = w_in1, w_out1, w_gate1, w_up1, w_down1
    q_g, k_g = l1_q_norm_g, l1_k_norm_g

    q, k, v, u = _inproj(h, mods, 0, l1_norm1_g, w_in, q_g, k_g, rope, tm=tm_in)
    kc, vc = _inproj(hc, mods, ctx_row, l1_norm1_g, w_in, q_g, k_g, None, tm=tm_in,
                     want_q=False, want_aux=False)
    kc, vc = kc.reshape(b, lc, -1), vc.reshape(b, lc, -1)

    attn = _attention(q, k, v, kc, vc, l1_sink, mode="window", tq=256, nsub=8, tk=512)
    pool_params = (bf(l1_pool_w), l1_pool_scale.reshape(1, -1))
    h, hn = _outproj(attn, u, "pool", pool_params, s, w_out, h, mods, 0, l1_norm2_g, tm=tm_out)
    return _ffn(hn, h, mods, 0, w_gate, w_up, w_down, final_norm_g, **ffn_tiles)
```

```python
import functools
import math

import jax
import jax.numpy as jnp
import numpy as np
from jax import lax
from jax.experimental import pallas as pl
from jax.experimental.pallas import tpu as pltpu

F32 = jnp.float32
BF16 = jnp.bfloat16

LANES = 128
HEAD_DIM = 128
AXIS_DIM = HEAD_DIM // 2
Q_HEADS = 12
KV_HEADS = 4
GROUPS = Q_HEADS // KV_HEADS
ATT_WIDTH = Q_HEADS * HEAD_DIM
KV_WIDTH = KV_HEADS * HEAD_DIM
GRID_W = 64
WINDOW = 128
POOL_WINDOWS = (2, 4, 8, 16)
ROPE_THETA = 10000.0
N_MOD = 6
EPS = 1e-6
NEG_INF = -1e30
ATT_SCALE = HEAD_DIM ** -0.5
LOG2E = math.log2(math.e)
MOD_ROWS = 8
VMEM_LIMIT = 56 * 1024 * 1024
NORM_ROWS = 16
CAST_CHUNK_ROWS = {2048: 512, 3072: 256, 5632: 128}
HALO = 8
SCORE_LOOKAHEAD = 2
VT_ROWS = HEAD_DIM + 16


def _params(sem):
    return pltpu.CompilerParams(dimension_semantics=sem, vmem_limit_bytes=VMEM_LIMIT)


def _mod_spec(row0, chunk):
    def imap(b, *_):
        return (b + row0, 0, chunk)
    return imap


def _const_spec(shape):
    zeros = (0,) * len(shape)
    return pl.BlockSpec(shape, lambda *_: zeros, pipeline_mode=pl.Buffered(1))


def _mod_kernel(c_ref, w_ref, b_ref, o_ref):
    c = c_ref[...]
    sc = c * jax.nn.sigmoid(c)
    o_ref[...] = jnp.dot(sc.astype(BF16), w_ref[...].astype(BF16),
                         preferred_element_type=F32) + b_ref[...]


def _modulation(c_all, w_mod, b_mod):
    d, n = w_mod.shape
    tn = 1024
    return pl.pallas_call(
        _mod_kernel,
        grid=(n // tn,),
        in_specs=[pl.BlockSpec((MOD_ROWS, d), lambda j: (0, 0)),
                  pl.BlockSpec((d, tn), lambda j: (0, j)),
                  pl.BlockSpec((1, tn), lambda j: (0, j))],
        out_specs=pl.BlockSpec((MOD_ROWS, tn), lambda j: (0, j)),
        out_shape=jax.ShapeDtypeStruct((MOD_ROWS, n), F32),
        name="modulation",
        compiler_params=_params(("arbitrary",)),
    )(c_all, w_mod, b_mod.reshape(1, n))


def _rms_modulate_rows(x, gs, shift):
    ms = jnp.mean(x * x, axis=-1, keepdims=True)
    return x * lax.rsqrt(ms + EPS) * gs + shift


def _norm_to(dst_ref, src, gs, shift, row0=0):
    rows = src.shape[0]
    for r in range(0, rows, NORM_ROWS):
        x = src[r:r + NORM_ROWS, :]
        dst_ref[row0 + r:row0 + r + NORM_ROWS, :] = _rms_modulate_rows(x, gs, shift).astype(BF16)


def _head_norm_rope(y, g, cos, sin):
    lane = lax.broadcasted_iota(jnp.int32, (1, HEAD_DIM), 1)
    in_x1 = (lane & (AXIS_DIM // 2)) == 0
    outs = []
    for hh in range(y.shape[1] // HEAD_DIM):
        t = y[:, hh * HEAD_DIM:(hh + 1) * HEAD_DIM]
        ms = jnp.mean(t * t, axis=-1, keepdims=True)
        t = t * lax.rsqrt(ms + EPS) * g
        if cos is not None:
            partner = jnp.where(in_x1, pltpu.roll(t, HEAD_DIM - AXIS_DIM // 2, 1), pltpu.roll(t, AXIS_DIM // 2, 1))
            t = t * cos + partner * sin
        outs.append(t)
    return jnp.concatenate(outs, axis=1)


def _inproj_kernel(*refs, tn, nq, rows, rope, want_q, want_aux):
    it = iter(refs)
    h_ref, shift_ref, scale_ref, g1_ref, wqk_ref, wr_ref, qg_ref, kg_ref = (next(it) for _ in range(8))
    cos_ref = sin_ref = None
    if rope:
        cos_ref, sin_ref = next(it), next(it)
    q_ref = next(it) if want_q else None
    k_ref, v_ref = next(it), next(it)
    aux_ref = next(it) if want_aux else None
    xn_ref = next(it)

    gs = g1_ref[...] * (1.0 + scale_ref[...])
    shift = shift_ref[...]
    qg = qg_ref[...] * (ATT_SCALE * LOG2E)
    kg = kg_ref[...]

    for r in range(0, h_ref.shape[0], rows):
        rs = slice(r, r + rows)
        _norm_to(xn_ref, h_ref.at[rs, :], gs, shift, row0=r)
        cos = cos_ref[rs, :] if rope else None
        sin = sin_ref[rs, :] if rope else None

        def tile(w_ref, j):
            return jnp.dot(xn_ref[rs, :], w_ref[:, j * tn:(j + 1) * tn], preferred_element_type=F32)

        if want_q:
            for j in range(nq):
                q_ref[rs, j * tn:(j + 1) * tn] = _head_norm_rope(tile(wqk_ref, j), qg, cos, sin).astype(BF16)
        k_ref[rs, :] = _head_norm_rope(tile(wqk_ref, nq), kg, cos, sin).astype(BF16)
        v_ref[rs, :] = tile(wr_ref, 0).astype(BF16)
        if want_aux:
            for j in range(aux_ref.shape[1] // tn):
                aux_ref[rs, j * tn:(j + 1) * tn] = tile(wr_ref, 1 + j)


def _inproj(h, mods3, row0, g1, w_in, q_g, k_g, rope_tabs, *, tm, want_q=True, want_aux=True):
    bx, lx, d = h.shape
    tn = KV_WIDTH
    nq = ATT_WIDTH // tn
    rope = rope_tabs is not None
    nqk = ATT_WIDTH + KV_WIDTH
    n_rest = w_in.shape[1] - nqk
    assert nqk % n_rest == 0
    aux_w = n_rest - KV_WIDTH
    row = lambda b, i: (b, i, 0)

    in_specs = [
        pl.BlockSpec((None, tm, d), row),
        pl.BlockSpec((None, 1, d), _mod_spec(row0, 0)),
        pl.BlockSpec((None, 1, d), _mod_spec(row0, 1)),
        _const_spec((1, d)),
        pl.BlockSpec((d, nqk), lambda b, i: (0, 0), pipeline_mode=pl.Buffered(1)),
        pl.BlockSpec((d, n_rest), lambda b, i: (0, nqk // n_rest), pipeline_mode=pl.Buffered(1)),
        _const_spec((1, HEAD_DIM)),
        _const_spec((1, HEAD_DIM)),
    ]
    args = [h, mods3, mods3, g1.reshape(1, d), w_in, w_in, q_g.reshape(1, HEAD_DIM), k_g.reshape(1, HEAD_DIM)]
    if rope:
        in_specs += [pl.BlockSpec((tm, HEAD_DIM), lambda b, i: (i, 0))] * 2
        args += list(rope_tabs)

    out_specs, out_shape = [], []
    if want_q:
        out_specs.append(pl.BlockSpec((None, tm, ATT_WIDTH), row))
        out_shape.append(jax.ShapeDtypeStruct((bx, lx, ATT_WIDTH), BF16))
    for _ in range(2):
        out_specs.append(pl.BlockSpec((None, tm, KV_WIDTH), row))
        out_shape.append(jax.ShapeDtypeStruct((bx, lx, KV_WIDTH), BF16))
    if want_aux:
        out_specs.append(pl.BlockSpec((None, tm, aux_w), row))
        out_shape.append(jax.ShapeDtypeStruct((bx, lx, aux_w), F32))

    return pl.pallas_call(
        functools.partial(_inproj_kernel, tn=tn, nq=nq, rows=tm // 2, rope=rope, want_q=want_q,
                          want_aux=want_aux),
        grid=(bx, lx // tm),
        in_specs=in_specs,
        out_specs=out_specs,
        out_shape=out_shape,
        scratch_shapes=[pltpu.VMEM((tm, d), BF16)],
        name="inproj",
        compiler_params=_params(("arbitrary", "arbitrary")),
    )(*args)


def _transpose_bf16(x):
    eye = (lax.broadcasted_iota(jnp.int32, (HEAD_DIM, HEAD_DIM), 0)
           == lax.broadcasted_iota(jnp.int32, (HEAD_DIM, HEAD_DIM), 1)).astype(BF16)
    return lax.dot_general(eye, x, (((1,), (1,)), ((), ())), preferred_element_type=F32).astype(BF16)


def _scores(kch, q, mask):
    s = lax.dot_general(kch, q, (((1,), (1,)), ((), ())), preferred_element_type=F32)
    return s if mask is None else jnp.where(mask, s, NEG_INF)


def _softmax_step(state, s, vt):
    m_prev, acc = state
    m_next = jnp.maximum(m_prev, jnp.max(s, axis=0, keepdims=True))
    p = jnp.exp2(s - m_next).astype(BF16)
    alpha = jnp.exp2(m_prev - m_next)
    acc = alpha * acc + jnp.dot(vt, p, preferred_element_type=F32)
    return m_next, acc


def _cast_classes(weights):
    cols = sorted({w.shape[1] for w in weights})
    return cols, [cols.index(w.shape[1]) for w in weights]


def _cast_job(t, w_f32, w_bf16, bufs_in, bufs_out, sem_in, sem_out):
    cols, cls = _cast_classes(w_f32)
    rows = [CAST_CHUNK_ROWS[w.shape[1]] for w in w_f32]
    counts = [w.shape[0] // r for w, r in zip(w_f32, rows)]
    first = [sum(counts[:a]) for a in range(len(counts))]
    slot = lax.rem(t, 2)

    def chunk(ref, a, c):
        return ref.at[pl.ds(pl.multiple_of(c * rows[a], rows[a]), rows[a]), :]

    def fetch(a, c, sl):
        return pltpu.make_async_copy(chunk(w_f32[a], a, c), bufs_in[cls[a]].at[sl], sem_in.at[cls[a], sl])

    def store(a, c, sl):
        return pltpu.make_async_copy(bufs_out[cls[a]].at[sl], chunk(w_bf16[a], a, c), sem_out.at[cls[a], sl])

    def is_job_of(tt, a):
        return (tt >= first[a]) & (tt < first[a] + counts[a])

    @pl.when(t == 0)
    def _():
        fetch(0, 0, 0).start()

    for a in range(len(counts)):
        @pl.when(is_job_of(t + 1, a))
        def _(a=a):
            fetch(a, t + 1 - first[a], 1 - slot).start(priority=1)

        @pl.when(is_job_of(t - 2, a))
        def _(a=a):
            store(a, t - 2 - first[a], slot).wait()

        @pl.when(is_job_of(t, a))
        def _(a=a):
            fetch(a, t - first[a], slot).wait()
            bufs_out[cls[a]][slot] = bufs_in[cls[a]][slot].astype(BF16)
            store(a, t - first[a], slot).start(priority=1)


def _attn_kernel(*refs, mode, tq, nsub, tk, seq, has_sink, n_cast):
    it = iter(refs)
    q_ref = next(it)
    k_ref = v_ref = None
    if mode != "ctx":
        k_ref, v_ref = next(it), next(it)
    kc_ref, vc_ref = next(it), next(it)
    sink_ref = next(it) if has_sink else None
    cast_in = [next(it) for _ in range(n_cast)]
    o_ref = next(it)
    cast_out = [next(it) for _ in range(n_cast)]
    vt_ref = next(it) if mode != "ctx" else None
    vct_ref = next(it)

    kvh = pl.program_id(1)
    step = pl.program_id(2)
    if n_cast:
        n_cls = len(_cast_classes(cast_in)[0])
        bufs_in = [next(it) for _ in range(n_cls)]
        bufs_out = [next(it) for _ in range(n_cls)]
        t = (pl.program_id(0) * pl.num_programs(1) + kvh) * pl.num_programs(2) + step
        _cast_job(t, cast_in, cast_out, bufs_in, bufs_out, next(it), next(it))

    @pl.when(step == 0)
    def _():
        vct_ref[:HEAD_DIM, :] = _transpose_bf16(vc_ref[...])
        vct_ref[HEAD_DIM:, :] = jnp.ones((VT_ROWS - HEAD_DIM, vct_ref.shape[1]), BF16)
        if mode != "ctx":
            vt_ref[HEAD_DIM:, :] = jnp.ones((VT_ROWS - HEAD_DIM, seq), BF16)
            for c in range(0, seq, tk):
                vt_ref[:HEAD_DIM, c:c + tk] = _transpose_bf16(v_ref[c:c + tk, :])

    n = GROUPS * tq
    if mode == "window":
        band0 = (lax.broadcasted_iota(jnp.int32, (tk, n), 0)
                 - (lax.broadcasted_iota(jnp.int32, (tk, n), 1) & (tq - 1)) + WINDOW)
        interior_mask = (band0 - WINDOW).astype(jnp.uint32) <= 2 * WINDOW

    items = []
    for sub in range(nsub):
        start = (step * nsub + sub) * tq
        rows = slice(sub * tq, (sub + 1) * tq)
        q = jnp.concatenate([q_ref[rows, g * HEAD_DIM:(g + 1) * HEAD_DIM] for g in range(GROUPS)], axis=0)
        if mode == "dense":
            for c in range(0, seq, tk):
                items.append((sub, q, None, (lambda c=c: k_ref[c:c + tk, :]), (lambda c=c: vt_ref[:, c:c + tk])))
        elif mode == "window":
            ks = pl.multiple_of(jnp.clip(start - WINDOW, 0, seq - tk), WINDOW)
            if 0 < sub < nsub - 1:
                mask = interior_mask
            else:
                mask = (band0 + (ks - start)).astype(jnp.uint32) <= 2 * WINDOW
            items.append((sub, q, mask, (lambda ks=ks: k_ref[pl.ds(ks, tk), :]),
                          (lambda ks=ks: vt_ref[:, pl.ds(ks, tk)])))
        items.append((sub, q, None, (lambda: kc_ref[...]), (lambda: vct_ref[...])))

    per_sub = len(items) // nsub
    items = [items[sub * per_sub + c] for c in range(per_sub) for sub in range(nsub)]

    states = [(jnp.full((1, n), NEG_INF, F32), jnp.zeros((VT_ROWS, n), F32)) for _ in range(nsub)]
    score = lambda item: _scores(item[3](), item[1], item[2])
    pending = [score(item) for item in items[:SCORE_LOOKAHEAD]]
    for idx, (sub, _, _, _, vt_fn) in enumerate(items):
        s_cur = pending.pop(0)
        if idx + SCORE_LOOKAHEAD < len(items):
            pending.append(score(items[idx + SCORE_LOOKAHEAD]))
        states[sub] = _softmax_step(states[sub], s_cur, vt_fn())

    for sub, (m, acc) in enumerate(states):
        l = acc[HEAD_DIM:HEAD_DIM + 1, :]
        acc = acc[:HEAD_DIM, :]
        if has_sink:
            sink = jnp.concatenate([jnp.full((1, tq), sink_ref[kvh * GROUPS + g] * LOG2E, F32)
                                    for g in range(GROUPS)], axis=1)
            l = l + jnp.exp2(sink - m)
        out = acc / l
        for g in range(GROUPS):
            o_ref[sub * tq:(sub + 1) * tq, g * HEAD_DIM:(g + 1) * HEAD_DIM] = (
                out[:, g * tq:(g + 1) * tq].T.astype(BF16))


def _attention(q, k, v, kc, vc, sink, *, mode, tq, nsub, tk, cast=()):
    b, lq, _ = q.shape
    lc = kc.shape[1]
    seq = k.shape[1] if k is not None else 0
    gw = GROUPS * HEAD_DIM
    tstep = tq * nsub
    in_specs = [pl.BlockSpec((None, tstep, gw), lambda bb, h, i: (bb, i, h))]
    args = [q]
    scratch = []
    if mode != "ctx":
        in_specs += [pl.BlockSpec((None, seq, HEAD_DIM), lambda bb, h, i: (bb, 0, h))] * 2
        args += [k, v]
        scratch.append(pltpu.VMEM((VT_ROWS, seq), BF16))
    in_specs += [pl.BlockSpec((None, lc, HEAD_DIM), lambda bb, h, i: (bb, 0, h))] * 2
    args += [kc, vc]
    scratch.append(pltpu.VMEM((VT_ROWS, lc), BF16))
    has_sink = sink is not None
    if has_sink:
        in_specs.append(pl.BlockSpec(memory_space=pltpu.SMEM))
        args.append(sink)
    out_specs = [pl.BlockSpec((None, tstep, gw), lambda bb, h, i: (bb, i, h))]
    out_shape = [jax.ShapeDtypeStruct((b, lq, ATT_WIDTH), BF16)]
    if cast:
        n_steps = b * KV_HEADS * (lq // tstep)
        n_jobs = sum(w.shape[0] // CAST_CHUNK_ROWS[w.shape[1]] for w in cast)
        assert n_jobs + 2 <= n_steps
        cols = _cast_classes(cast)[0]
        in_specs += [pl.BlockSpec(memory_space=pl.ANY)] * len(cast)
        args += list(cast)
        out_specs += [pl.BlockSpec(memory_space=pl.ANY)] * len(cast)
        out_shape += [jax.ShapeDtypeStruct(w.shape, BF16) for w in cast]
        scratch += [pltpu.VMEM((2, CAST_CHUNK_ROWS[c], c), F32) for c in cols]
        scratch += [pltpu.VMEM((2, CAST_CHUNK_ROWS[c], c), BF16) for c in cols]
        scratch += [pltpu.SemaphoreType.DMA((len(cols), 2)), pltpu.SemaphoreType.DMA((len(cols), 2))]
    outs = pl.pallas_call(
        functools.partial(_attn_kernel, mode=mode, tq=tq, nsub=nsub, tk=tk, seq=seq, has_sink=has_sink,
                          n_cast=len(cast)),
        grid=(b, KV_HEADS, lq // tstep),
        in_specs=in_specs,
        out_specs=out_specs,
        out_shape=out_shape,
        scratch_shapes=scratch,
        name="attn_" + mode,
        compiler_params=_params(("arbitrary", "arbitrary", "arbitrary")),
    )(*args)
    return (outs[0], outs[1:]) if cast else outs[0]


def _conv_mixer(g, aux_ref, prev_ref, next_ref, w_ref, row0, seq_len):
    tm, w = aux_ref.shape[0], aux_ref.shape[1] // 3
    gb, gc, uu = (slice(k * w + g * LANES, k * w + (g + 1) * LANES) for k in range(3))
    z = aux_ref[:, gc] * aux_ref[:, uu]
    z_prev = prev_ref[HALO - 1:HALO, gc] * prev_ref[HALO - 1:HALO, uu]
    z_next = next_ref[0:1, gc] * next_ref[0:1, uu]
    row = lax.broadcasted_iota(jnp.int32, (tm, LANES), 0)
    pos = (row0 + row) & (seq_len - 1)
    down = jnp.where(row == 0, z_prev, pltpu.roll(z, 1, 0))
    down = jnp.where(pos == 0, 0.0, down)
    up = jnp.where(row == tm - 1, z_next, pltpu.roll(z, tm - 1, 0))
    up = jnp.where(pos == seq_len - 1, 0.0, up)
    cw = w_ref[:, g * LANES:(g + 1) * LANES]
    return aux_ref[:, gb] * (cw[0:1, :] * down + cw[1:2, :] * z + cw[2:3, :] * up)


def _pool_mixer(g, u_ref, prev_ref, next_ref, w_ref, s_ref, row0, seq_len):
    tm = u_ref.shape[0]
    at_start = (row0 & (seq_len - 1)) == 0
    at_end = ((row0 + tm) & (seq_len - 1)) == 0
    pos = (row0 & (seq_len - 1)) + lax.broadcasted_iota(jnp.int32, (tm, LANES), 0)
    cols = slice(g * LANES, (g + 1) * LANES)
    u = u_ref[:, cols]
    ext = jnp.concatenate([jnp.where(at_start, 0.0, prev_ref[:, cols]), u,
                           jnp.where(at_end, 0.0, next_ref[:, cols])], axis=0)
    n = ext.shape[0]
    half = POOL_WINDOWS[g] // 2
    fwd = ext
    bwd = pltpu.roll(ext, 1, 0)
    k = 1
    while k < half:
        fwd = fwd + pltpu.roll(fwd, n - k, 0)
        bwd = bwd + pltpu.roll(bwd, k, 0)
        k *= 2
    cnt = jnp.minimum(pos + half, seq_len) - jnp.maximum(pos - half, 0)
    pooled = (fwd + bwd)[HALO:HALO + tm, :] / cnt.astype(F32) - u
    mixed = jnp.dot(pooled.astype(BF16), w_ref[g], preferred_element_type=F32)
    return mixed * s_ref[:, cols]


def _outproj_kernel(*refs, mixer, rows, seq_len):
    it = iter(refs)
    a_ref, aux_ref, prev_ref, next_ref = (next(it) for _ in range(4))
    mix_refs = [next(it) for _ in range(1 if mixer == "conv" else 2)]
    wa_ref, wx_ref, h_ref, gate_ref, shift_ref, scale_ref, g2_ref, o_ref, hn_ref, x_ref = it

    row0 = pl.program_id(1) * a_ref.shape[0]
    mix = _conv_mixer if mixer == "conv" else _pool_mixer
    gate = gate_ref[...]
    gs = g2_ref[...] * (1.0 + scale_ref[...])
    shift = shift_ref[...]
    n_groups = x_ref.shape[1] // LANES
    ncols = o_ref.shape[1] // n_groups
    for r in range(0, a_ref.shape[0], rows):
        rs = slice(r, r + rows)
        for g in range(n_groups):
            cs = slice(g * ncols, (g + 1) * ncols)
            o_ref[rs, cs] = h_ref[rs, cs] + gate[:, cs] * jnp.dot(a_ref[rs, :], wa_ref[:, cs],
                                                               preferred_element_type=F32)
            if r == 0:
                x_ref[:, g * LANES:(g + 1) * LANES] = mix(g, aux_ref, prev_ref, next_ref, *mix_refs,
                                                          row0, seq_len).astype(BF16)
        o_ref[rs, :] += gate * jnp.dot(x_ref[rs, :], wx_ref[...], preferred_element_type=F32)
        _norm_to(hn_ref, o_ref.at[rs, :], gs, shift, row0=r)


def _outproj(attn, aux, mixer, mix_params, seq_len, w_out, h, mods3, row0, g2, *, tm):
    bx, lx, d = h.shape
    wa, aux_w = attn.shape[2], aux.shape[2]
    wx = d - wa
    row = lambda b, i: (b, i, 0)
    halo_blocks = tm // HALO
    mix_specs = [_const_spec(p.shape) for p in mix_params]
    return pl.pallas_call(
        functools.partial(_outproj_kernel, mixer=mixer, rows=tm // 2, seq_len=seq_len),
        grid=(bx, lx // tm),
        in_specs=[pl.BlockSpec((None, tm, wa), row),
                  pl.BlockSpec((None, tm, aux_w), row),
                  pl.BlockSpec((None, HALO, aux_w), lambda b, i: (b, jnp.maximum(i * halo_blocks - 1, 0), 0)),
                  pl.BlockSpec((None, HALO, aux_w),
                               lambda b, i: (b, jnp.minimum((i + 1) * halo_blocks, lx // HALO - 1), 0)),
                  *mix_specs,
                  pl.BlockSpec((wa, d), lambda b, i: (0, 0), pipeline_mode=pl.Buffered(1)),
                  pl.BlockSpec((wx, d), lambda b, i: (wa // wx, 0), pipeline_mode=pl.Buffered(1)),
                  pl.BlockSpec((None, tm, d), row),
                  pl.BlockSpec((None, 1, d), _mod_spec(row0, 2)),
                  pl.BlockSpec((None, 1, d), _mod_spec(row0, 3)),
                  pl.BlockSpec((None, 1, d), _mod_spec(row0, 4)),
                  _const_spec((1, d))],
        out_specs=[pl.BlockSpec((None, tm, d), row), pl.BlockSpec((None, tm, d), row)],
        out_shape=[jax.ShapeDtypeStruct((bx, lx, d), F32), jax.ShapeDtypeStruct((bx, lx, d), BF16)],
        scratch_shapes=[pltpu.VMEM((tm, wx), BF16)],
        name="outproj_" + mixer,
        compiler_params=_params(("arbitrary", "arbitrary")),
    )(attn, aux, aux, aux, *mix_params, w_out, w_out, h, mods3, mods3, mods3, g2.reshape(1, d))


def _ffn_kernel(*refs, final, th, rows, ncols):
    it = iter(refs)
    hn_ref, h_hbm, gate_ref, wg_hbm, wu_hbm, wd_hbm = (next(it) for _ in range(6))
    fg_ref = next(it) if final else None
    o_ref, wg_buf, wu_buf, wd_buf, h_buf, w_sem, h_sem = (next(it) for _ in range(7))

    b, i = pl.program_id(0), pl.program_id(1)
    tm, d = o_ref.shape
    n_hid = wg_hbm.shape[1] // th
    tile = b * pl.num_programs(1) + i
    n_steps = pl.num_programs(0) * pl.num_programs(1) * n_hid
    step0 = tile * n_hid

    def w_copies(j, slot):
        col = j * th if isinstance(j, int) else pl.multiple_of(j * th, th)
        return (pltpu.make_async_copy(wg_hbm.at[:, pl.ds(col, th)], wg_buf.at[slot], w_sem.at[0, slot]),
                pltpu.make_async_copy(wu_hbm.at[:, pl.ds(col, th)], wu_buf.at[slot], w_sem.at[1, slot]),
                pltpu.make_async_copy(wd_hbm.at[pl.ds(col, th), :], wd_buf.at[slot], w_sem.at[2, slot]))

    h_copy = pltpu.make_async_copy(h_hbm.at[pl.ds(pl.multiple_of(tile * tm, tm), tm), :], h_buf, h_sem)
    h_copy.start()

    @pl.when(tile == 0)
    def _():
        for cp in w_copies(0, 0):
            cp.start()

    gate = gate_ref[...]

    def hidden_step(j, first=False, last=False):
        step = step0 + j
        slot = lax.rem(step, 2)
        for cp in w_copies(j, slot):
            cp.wait()

        @pl.when(step + 1 < n_steps)
        def _():
            nxt = jnp.where(j + 1 < n_hid, j + 1, 0)
            for cp in w_copies(nxt, 1 - slot):
                cp.start()

        wg, wu, wd = wg_buf.at[slot], wu_buf.at[slot], wd_buf.at[slot]
        for r in range(0, tm, rows):
            rs = slice(r, r + rows)
            hn = hn_ref[rs, :]
            gt = jnp.dot(hn, wg[...], preferred_element_type=F32)
            up = jnp.dot(hn, wu[...], preferred_element_type=F32)
            act = (gt * jax.nn.sigmoid(gt) * up).astype(BF16)
            if last and r == 0:
                h_copy.wait()
            for c in range(0, d, ncols):
                cs = slice(c, c + ncols)
                dn = gate[:, cs] * jnp.dot(act, wd[:, cs], preferred_element_type=F32)
                if first:
                    o_ref[rs, cs] = dn
                elif last:
                    o_ref[rs, cs] += dn + h_buf[rs, cs]
                else:
                    o_ref[rs, cs] += dn

    hidden_step(0, first=True)

    def loop_body(j, carry):
        hidden_step(j)
        return carry

    lax.fori_loop(1, n_hid - 1, loop_body, 0)
    hidden_step(n_hid - 1, last=True)

    if final:
        fg = fg_ref[...]
        for r in range(0, tm, NORM_ROWS):
            x = o_ref[r:r + NORM_ROWS, :]
            ms = jnp.mean(x * x, axis=-1, keepdims=True)
            o_ref[r:r + NORM_ROWS, :] = x * lax.rsqrt(ms + EPS) * fg


def _ffn(hn, h, mods3, row0, w_gate, w_up, w_down, final_g, *, tm, th, rows, ncols):
    bx, lx, d = h.shape
    final = final_g is not None
    row = lambda b, i: (b, i, 0)
    any_spec = pl.BlockSpec(memory_space=pl.ANY)
    in_specs = [pl.BlockSpec((None, tm, d), row), any_spec,
                pl.BlockSpec((None, 1, d), _mod_spec(row0, 5)),
                any_spec, any_spec, any_spec]
    args = [hn, h.reshape(bx * lx, d), mods3, w_gate, w_up, w_down]
    if final:
        in_specs.append(_const_spec((1, d)))
        args.append(final_g.reshape(1, d))
    return pl.pallas_call(
        functools.partial(_ffn_kernel, final=final, th=th, rows=rows, ncols=ncols),
        grid=(bx, lx // tm),
        in_specs=in_specs,
        out_specs=pl.BlockSpec((tm, d), lambda b, i: (b * (lx // tm) + i, 0)),
        out_shape=jax.ShapeDtypeStruct((bx * lx, d), F32),
        scratch_shapes=[pltpu.VMEM((2, d, th), BF16), pltpu.VMEM((2, d, th), BF16), pltpu.VMEM((2, th, d), BF16),
                        pltpu.VMEM((tm, d), F32), pltpu.SemaphoreType.DMA((3, 2)), pltpu.SemaphoreType.DMA(())],
        name="ffn",
        compiler_params=_params(("arbitrary", "arbitrary")),
    )(*args).reshape(bx, lx, d)


def _rope_tables(n):
    pos = np.arange(n)
    inv = np.power(ROPE_THETA, -np.arange(0, AXIS_DIM, 2, dtype=np.float64) / AXIS_DIM)
    ang_r = (pos // GRID_W)[:, None] * inv
    ang_c = (pos % GRID_W)[:, None] * inv
    cos = np.concatenate([np.cos(ang_r), np.cos(ang_r), np.cos(ang_c), np.cos(ang_c)], axis=-1)
    sin = np.concatenate([-np.sin(ang_r), np.sin(ang_r), -np.sin(ang_c), np.sin(ang_c)], axis=-1)
    return jnp.asarray(cos, F32), jnp.asarray(sin, F32)


def kernel(x, c, ctx, c_ctx, l0_norm1_g, l0_w_mod, l0_b_mod, l0_w_in, l0_q_norm_g, l0_k_norm_g, l0_conv_w, l0_w_out, l0_norm2_g, l0_w_gate, l0_w_up, l0_w_down, l1_norm1_g, l1_w_mod, l1_b_mod, l1_w_in, l1_q_norm_g, l1_k_norm_g, l1_sink, l1_pool_w, l1_pool_scale, l1_w_out, l1_norm2_g, l1_w_gate, l1_w_up, l1_w_down, final_norm_g):
    b, s, d = x.shape
    lc = ctx.shape[1]
    rope = _rope_tables(s)
    c_all = jnp.zeros((MOD_ROWS, d), F32).at[:b].set(c).at[b].set(c_ctx)
    ctx_row = b
    bf = lambda w: w.astype(BF16)
    tm_in, tm_out = 512, 512
    ffn_tiles = dict(tm=1024, th=512, rows=512, ncols=512)

    mods = _modulation(c_all, l0_w_mod, l0_b_mod).reshape(MOD_ROWS, 1, N_MOD * d)
    w_in = bf(l0_w_in)
    q_g, k_g = l0_q_norm_g, l0_k_norm_g
    hc = ctx.reshape(1, b * lc, d)

    q, k, v, aux = _inproj(x, mods, 0, l0_norm1_g, w_in, q_g, k_g, rope, tm=tm_in)
    qc, kc, vc, auxc = _inproj(hc, mods, ctx_row, l0_norm1_g, w_in, q_g, k_g, None, tm=tm_in)
    qc, kc, vc = (t.reshape(b, lc, -1) for t in (qc, kc, vc))

    later = (l0_w_out, l0_w_gate, l0_w_up, l0_w_down, l1_w_in, l1_w_out, l1_w_gate, l1_w_up, l1_w_down)
    attn, casted = _attention(q, k, v, kc, vc, None, mode="dense", tq=256, nsub=2, tk=512, cast=later)
    w_out, w_gate, w_up, w_down, w_in1, w_out1, w_gate1, w_up1, w_down1 = casted
    h, hn = _outproj(attn, aux, "conv", (l0_conv_w,), s, w_out, x, mods, 0, l0_norm2_g, tm=tm_out)
    h = _ffn(hn, h, mods, 0, w_gate, w_up, w_down, None, **ffn_tiles)

    attn_c = _attention(qc, None, None, kc, vc, None, mode="ctx", tq=lc, nsub=1, tk=lc)
    hc, hcn = _outproj(attn_c.reshape(1, b * lc, -1), auxc, "conv", (l0_conv_w,), lc, w_out, hc, mods, ctx_row,
                       l0_norm2_g, tm=tm_out)
    hc = _ffn(hcn, hc, mods, ctx_row, w_gate, w_up, w_down, None, **ffn_tiles)

    mods = _modulation(c_all, l1_w_mod, l1_b_mod).reshape(MOD_ROWS, 1, N_MOD * d)
    w_in, w_out, w_gate, w_up, w_down = w_in1, w_out1, w_gate1, w_up1, w_down1
    q_g, k_g = l1_q_norm_g, l1_k_norm_g

    q, k, v, u = _inproj(h, mods, 0, l1_norm1_g, w_in, q_g, k_g, rope, tm=tm_in)
    kc, vc = _inproj(hc, mods, ctx_row, l1_norm1_g, w_in, q_g, k_g, None, tm=tm_in,
                     want_q=False, want_aux=False)
    kc, vc = kc.reshape(b, lc, -1), vc.reshape(b, lc, -1)

    attn = _attention(q, k, v, kc, vc, l1_sink, mode="window", tq=256, nsub=8, tk=512)
    pool_params = (bf(l1_pool_w), l1_pool_scale.reshape(1, -1))
    h, hn = _outproj(attn, u, "pool", pool_params, s, w_out, h, mods, 0, l1_norm2_g, tm=tm_out)
    return _ffn(hn, h, mods, 0, w_gate, w_up, w_down, final_norm_g, **ffn_tiles)
```
